```python
import jax, jax.numpy as jnp
from jax import lax
import numpy as np

D_MODEL = 1024
BATCH = 16
SEQ = 2048
DEPTH = 1

GRID_W = 64
CTX_LEN = 256

MLA_HEADS = 8
MLA_Q_RANK = 256
MLA_KV_RANK = 128
MLA_NOPE = 64
MLA_ROPE = 32
MLA_V = 64
MLA_WIDTH = MLA_HEADS * MLA_V

HG_HEADS = 4
HG_DK = 128
HG_DV = 128
HG_KW = HG_HEADS * HG_DK
HG_WIDTH = HG_HEADS * HG_DV

MIX_WIDTH = MLA_WIDTH + HG_WIDTH
D_FF = 4 * D_MODEL
ROPE_BASE = 10000.0
Q_BLOCK = 128
CHUNK = 64
EPS = 1e-6

IN_SPLITS = (MLA_Q_RANK, MLA_KV_RANK, MLA_ROPE,
             HG_KW, HG_KW, HG_KW, HG_WIDTH, HG_WIDTH)
IN_WIDTH = sum(IN_SPLITS)

kernel_name = "hymba_mla_hgrn2_dit_block"


def rmsnorm(x, g):
    xf = x.astype(jnp.float32)
    xf = xf * lax.rsqrt(jnp.mean(xf * xf, axis=-1, keepdims=True) + EPS)
    return xf.astype(x.dtype) * g


def modulate(h, shift, scale):
    return h * (1 + scale) + shift


def split_cols(p):
    out, start = [], 0
    for n in IN_SPLITS:
        out.append(p[..., start:start + n])
        start += n
    return out


def axial_angles(T):
    rows = T // GRID_W
    row = jnp.repeat(jnp.arange(rows), GRID_W).astype(jnp.float32)
    col = jnp.tile(jnp.arange(GRID_W), rows).astype(jnp.float32)
    n = MLA_ROPE // 4
    inv = ROPE_BASE ** (-jnp.arange(n, dtype=jnp.float32) / n)
    return row[:, None] * inv, col[:, None] * inv


def rotate_half_pairs(x, ang):
    n = x.shape[-1] // 2
    cos = jnp.cos(ang)[:, None, :].astype(x.dtype)
    sin = jnp.sin(ang)[:, None, :].astype(x.dtype)
    x1, x2 = x[..., :n], x[..., n:]
    return jnp.concatenate([x1 * cos - x2 * sin, x2 * cos + x1 * sin], axis=-1)


def rope2d(x, ang_r, ang_c):
    half = MLA_ROPE // 2
    return jnp.concatenate([rotate_half_pairs(x[..., :half], ang_r),
                            rotate_half_pairs(x[..., half:], ang_c)], axis=-1)


def mla_attend(q_nope, q_rope, k_nope, k_rope, v):
    scale = (MLA_NOPE + MLA_ROPE) ** -0.5
    s = (jnp.einsum('bqhd,bkhd->bhqk', q_nope, k_nope)
         + jnp.einsum('bqhr,bkr->bhqk', q_rope, k_rope))
    p = jax.nn.softmax(s.astype(jnp.float32) * scale, axis=-1).astype(v.dtype)
    return jnp.einsum('bhqk,bkhd->bqhd', p, v)


def gated_chunk_scan(q, log_f, k, v, s0):
    B, T, H, _ = q.shape
    DV = v.shape[-1]
    n = T // CHUNK

    def to_chunks(a):
        return a.astype(jnp.float32).reshape(B, n, CHUNK, H, a.shape[-1]).transpose(1, 0, 3, 2, 4)

    mask = jnp.tril(jnp.ones((CHUNK, CHUNK), dtype=bool))[:, :, None]

    def step(S, inp):
        qc, lfc, kc, vc = inp
        b = jnp.cumsum(lfc, axis=2)
        diff = b[:, :, :, None, :] - b[:, :, None, :, :]
        decay = jnp.exp(jnp.where(mask, diff, -jnp.inf))
        a = jnp.einsum('bhtk,bhtsk,bhsk->bhts', qc, decay, kc)
        o = (jnp.einsum('bhts,bhsv->bhtv', a, vc)
             + jnp.einsum('bhtk,bhkv->bhtv', qc * jnp.exp(b), S))
        b_last = b[:, :, -1:, :]
        S = (jnp.exp(b_last[:, :, 0, :])[..., None] * S
             + jnp.einsum('bhsk,bhsv->bhkv', kc * jnp.exp(b_last - b), vc))
        return S, o

    S, o = lax.scan(step, s0, (to_chunks(q), to_chunks(log_f), to_chunks(k), to_chunks(v)))
    o = o.transpose(1, 0, 3, 2, 4).reshape(B, T, H, DV)
    return o, S


def hgrn2_direction(q, zf, i, qc, zfc, ic, lb, reverse):
    def gates(z):
        sig_pos = jax.nn.sigmoid(z.astype(jnp.float32))
        log_f = jnp.log(lb + (1 - lb) * sig_pos)
        k = (1 - lb) * jax.nn.sigmoid(-z.astype(jnp.float32))
        return log_f, k

    lf, k = gates(zf)
    lfc, kc = gates(zfc)
    if reverse:
        q, lf, k, i = [jnp.flip(a, axis=1) for a in (q, lf, k, i)]
        qc, lfc, kc, ic = [jnp.flip(a, axis=1) for a in (qc, lfc, kc, ic)]
    B = q.shape[0]
    s0 = jnp.zeros((B, HG_HEADS, HG_DK, HG_DV), jnp.float32)
    o_c, s_c = gated_chunk_scan(qc, lfc, kc, ic, s0)
    o, _ = gated_chunk_scan(q, lf, k, i, s_c)
    if reverse:
        o, o_c = jnp.flip(o, axis=1), jnp.flip(o_c, axis=1)
    return o, o_c


def mixer(h, hc, ang_r, ang_c, w_in, q_norm, w_uq, kv_norm, w_ukv, lb, hg_norm, w_out, need_ctx):
    B, T, _ = h.shape
    L = hc.shape[1]
    cq, ckv, kr, hq, hf, hb, hi, hg = split_cols(h @ w_in)
    ccq, cckv, ckr, chq, chf, chb, chi, chg = split_cols(hc @ w_in)

    def mla_qkv(cq_, ckv_, kr_, S):
        q = (rmsnorm(cq_, q_norm) @ w_uq).reshape(B, S, MLA_HEADS, MLA_NOPE + MLA_ROPE)
        kv = (rmsnorm(ckv_, kv_norm) @ w_ukv).reshape(B, S, MLA_HEADS, MLA_NOPE + MLA_V)
        return q[..., :MLA_NOPE], q[..., MLA_NOPE:], kv[..., :MLA_NOPE], kv[..., MLA_NOPE:], kr_

    q_nope, q_rope, k_nope, v, k_rope = mla_qkv(cq, ckv, kr, T)
    q_rope = rope2d(q_rope, ang_r, ang_c)
    k_rope = rope2d(k_rope[:, :, None, :], ang_r, ang_c)[:, :, 0, :]
    cq_nope, cq_rope, ck_nope, cv, ck_rope = mla_qkv(ccq, cckv, ckr, L)

    k_all = jnp.concatenate([k_nope, ck_nope], axis=1)
    kr_all = jnp.concatenate([k_rope, ck_rope], axis=1)
    v_all = jnp.concatenate([v, cv], axis=1)

    nb = T // Q_BLOCK

    def blocks(a):
        return a.reshape(B, nb, Q_BLOCK, *a.shape[2:]).swapaxes(0, 1)

    o_mla = lax.map(lambda qs: mla_attend(qs[0], qs[1], k_all, kr_all, v_all),
                    (blocks(q_nope), blocks(q_rope)))
    o_mla = o_mla.swapaxes(0, 1).reshape(B, T, MLA_WIDTH)

    def hg_heads(a, S, d):
        return a.reshape(B, S, HG_HEADS, d)

    q_h, cq_h = jax.nn.silu(hg_heads(hq, T, HG_DK)), jax.nn.silu(hg_heads(chq, L, HG_DK))
    i_h, ci_h = hg_heads(hi, T, HG_DV), hg_heads(chi, L, HG_DV)
    lb_f = lb[0].reshape(HG_HEADS, HG_DK)
    lb_b = lb[1].reshape(HG_HEADS, HG_DK)
    o_f, oc_f = hgrn2_direction(q_h, hg_heads(hf, T, HG_DK), i_h, cq_h, hg_heads(chf, L, HG_DK), ci_h, lb_f, False)
    o_b, oc_b = hgrn2_direction(q_h, hg_heads(hb, T, HG_DK), i_h, cq_h, hg_heads(chb, L, HG_DK), ci_h, lb_b, True)
    o_hg = rmsnorm(o_f + o_b, hg_norm) * jax.nn.silu(hg_heads(hg, T, HG_DV).astype(jnp.float32))
    o_hg = o_hg.reshape(B, T, HG_WIDTH).astype(h.dtype)

    y = jnp.concatenate([o_mla, o_hg], axis=-1) @ w_out

    if not need_ctx:
        return y, None
    oc_mla = mla_attend(cq_nope, cq_rope, ck_nope, ck_rope, cv).reshape(B, L, MLA_WIDTH)
    oc_hg = rmsnorm(oc_f + oc_b, hg_norm) * jax.nn.silu(hg_heads(chg, L, HG_DV).astype(jnp.float32))
    oc_hg = oc_hg.reshape(B, L, HG_WIDTH).astype(hc.dtype)
    yc = jnp.concatenate([oc_mla, oc_hg], axis=-1) @ w_out
    return y, yc


def sq_relu_mlp(h, w1, w2):
    return jnp.square(jax.nn.relu(h @ w1)) @ w2


def setup_inputs(seed: int = 0) -> dict:
    key = jax.random.key(seed)
    ks = jax.random.split(key, 20)
    f32 = jnp.float32

    def nrm(k, shape, scale):
        return jax.random.normal(k, shape, f32) * scale

    def gain(k, shape):
        return 1.0 + 0.02 * jax.random.normal(k, shape, f32)

    return {
        "x": nrm(ks[0], (BATCH, SEQ, D_MODEL), 1.0),
        "c": nrm(ks[1], (BATCH, D_MODEL), 1.0),
        "ctx": nrm(ks[2], (BATCH, CTX_LEN, D_MODEL), 1.0),
        "c_ctx": nrm(ks[3], (D_MODEL,), 1.0),
        "w_ada": nrm(ks[4], (DEPTH, D_MODEL, 6 * D_MODEL), 0.01),
        "b_ada": nrm(ks[5], (DEPTH, 6 * D_MODEL), 0.01),
        "norm_mix": gain(ks[6], (DEPTH, D_MODEL)),
        "w_in": nrm(ks[7], (DEPTH, D_MODEL, IN_WIDTH), D_MODEL ** -0.5),
        "q_norm": gain(ks[8], (DEPTH, MLA_Q_RANK)),
        "w_uq": nrm(ks[9], (DEPTH, MLA_Q_RANK, MLA_HEADS * (MLA_NOPE + MLA_ROPE)), MLA_Q_RANK ** -0.5),
        "kv_norm": gain(ks[10], (DEPTH, MLA_KV_RANK)),
        "w_ukv": nrm(ks[11], (DEPTH, MLA_KV_RANK, MLA_HEADS * (MLA_NOPE + MLA_V)), MLA_KV_RANK ** -0.5),
        "hgrn_lb": nrm(ks[12], (DEPTH + 1, 2, HG_KW), 0.1),
        "hgrn_norm": gain(ks[13], (DEPTH, HG_DV)),
        "w_out": nrm(ks[14], (DEPTH, MIX_WIDTH, D_MODEL), MIX_WIDTH ** -0.5),
        "norm_mlp": gain(ks[15], (DEPTH, D_MODEL)),
        "w_mlp_in": nrm(ks[16], (DEPTH, D_MODEL, D_FF), D_MODEL ** -0.5),
        "w_mlp_out": nrm(ks[17], (DEPTH, D_FF, D_MODEL), D_FF ** -0.5),
        "final_norm": gain(ks[18], (D_MODEL,)),
    }


def reference(x, c, ctx, c_ctx, w_ada, b_ada, norm_mix, w_in, q_norm, w_uq, kv_norm, w_ukv,
              hgrn_lb, hgrn_norm, w_out, norm_mlp, w_mlp_in, w_mlp_out, final_norm):
    T = x.shape[1]
    ang_r, ang_c = axial_angles(T)
    lb_all = jnp.cumsum(jax.nn.softmax(hgrn_lb.astype(jnp.float32), axis=0), axis=0)
    s_lat = jax.nn.silu(c)
    s_ctx = jax.nn.silu(c_ctx)
    z = ctx
    for l in range(DEPTH):
        need_ctx = l < DEPTH - 1
        mod = s_lat @ w_ada[l] + b_ada[l]
        mod_c = s_ctx @ w_ada[l] + b_ada[l]
        sh1, sc1, g1, sh2, sc2, g2 = [m[:, None, :] for m in jnp.split(mod, 6, axis=-1)]
        csh1, csc1, cg1, csh2, csc2, cg2 = jnp.split(mod_c, 6, axis=-1)

        h = modulate(rmsnorm(x, norm_mix[l]), sh1, sc1)
        hc = modulate(rmsnorm(z, norm_mix[l]), csh1, csc1)
        y, yc = mixer(h, hc, ang_r, ang_c, w_in[l], q_norm[l], w_uq[l], kv_norm[l], w_ukv[l],
                      lb_all[l], hgrn_norm[l], w_out[l], need_ctx)
        x = x + g1 * y
        x = x + g2 * sq_relu_mlp(modulate(rmsnorm(x, norm_mlp[l]), sh2, sc2), w_mlp_in[l], w_mlp_out[l])
        if need_ctx:
            z = z + cg1 * yc
            z = z + cg2 * sq_relu_mlp(modulate(rmsnorm(z, norm_mlp[l]), csh2, csc2), w_mlp_in[l], w_mlp_out[l])
    return rmsnorm(x, final_norm)
```

```python
import functools

import jax
import jax.numpy as jnp
from jax import lax
from jax.experimental import pallas as pl
from jax.experimental.pallas import tpu as pltpu

F32 = jnp.float32
BF16 = jnp.bfloat16

GRID_W = 64
MLA_HEADS = 8
MLA_Q_RANK = 256
MLA_KV_RANK = 128
MLA_NOPE = 64
MLA_ROPE = 32
MLA_V = 64
HG_HEADS = 4
HG_DK = 128
HG_DV = 128
HG_KW = HG_HEADS * HG_DK
ROPE_BASE = 10000.0
EPS = 1e-6

LANES = 128
HEAD_SLOT = LANES
ROW_TILE = 256
MLP_ROW_TILE = 512
VMEM_LIMIT = 56 * 1024 * 1024

_C_CQ = (0, 256)
_C_CKV = (256, 384)
_C_KR = (384, 512)
_C_KRR = (512, 640)
_C_HQ = (640, 1152)
_C_HF = (1152, 1664)
_C_HB = (1664, 2176)
_C_HI = (2176, 2688)
_C_HG = (2688, 3200)
IN_P_WIDTH = 3200


def _rms(x, g):
    ms = jnp.mean(x * x, axis=-1, keepdims=True)
    return (x * lax.rsqrt(ms + EPS)) * g


def _silu(x):
    return x * jax.nn.sigmoid(x)


def _dot(a, b):
    return jnp.dot(a, b, preferred_element_type=F32)


def _dot_nt(a, b):
    return lax.dot_general(a, b, (((1,), (1,)), ((), ())), preferred_element_type=F32)


def _dot_tn(a, b):
    return lax.dot_general(a, b, (((0,), (0,)), ((), ())), preferred_element_type=F32)


def _ada_kernel(c_ref, w_ref, b_ref, o_ref):
    s = _silu(c_ref[...])
    o_ref[...] = jnp.dot(s, w_ref[...], preferred_element_type=F32,
                         precision=lax.Precision.HIGHEST) + b_ref[...]


def _ada_call(cc, w_ada, b_ada):
    rows, d = cc.shape
    n = w_ada.shape[1]
    tn = 1024
    return pl.pallas_call(
        _ada_kernel,
        grid=(n // tn,),
        in_specs=[
            pl.BlockSpec((rows, d), lambda j: (0, 0)),
            pl.BlockSpec((d, tn), lambda j: (0, j)),
            pl.BlockSpec((1, tn), lambda j: (0, j)),
        ],
        out_specs=pl.BlockSpec((rows, tn), lambda j: (0, j)),
        out_shape=jax.ShapeDtypeStruct((rows, n), F32),
        compiler_params=pltpu.CompilerParams(dimension_semantics=("arbitrary",), vmem_limit_bytes=VMEM_LIMIT),
        name="ada_mod",
    )(cc, w_ada, b_ada)


def _inproj_kernel(n_lat, x_ref, ctx_ref, mod_ref, nmix_ref, win_ref, qn_ref, wq_ref, kvn_ref, wkv_ref,
                   lb_ref, cos_ref, sin_ref,
                   q_out, k_out, v_out, qh_out, lff_out, lfb_out, vi_out, g_out):
    i = pl.program_id(1)
    is_lat = i < n_lat
    xin = jnp.where(is_lat, x_ref[0], ctx_ref[0])
    mod = mod_ref[0, 0]
    h = _rms(xin, nmix_ref[...]) * (1.0 + mod[1:2]) + mod[0:1]
    hb = h.astype(BF16)

    def proj(c):
        return _dot(hb, win_ref[:, c[0]:c[1]])

    cos = cos_ref[...]
    sin = sin_ref[...]
    cos8 = jnp.tile(cos, (1, MLA_HEADS))
    sin8 = jnp.tile(sin, (1, MLA_HEADS))
    nq = MLA_HEADS * HEAD_SLOT

    cqn = _rms(proj(_C_CQ), qn_ref[...]).astype(BF16)
    qq = _dot(cqn, wq_ref[...])
    scale = (MLA_NOPE + MLA_ROPE) ** -0.5
    q = (qq[:, :nq] * cos8 + qq[:, nq:] * sin8) * scale

    @pl.when(is_lat)
    def _():
        q_out[0] = q.astype(BF16)

    ckvn = _rms(proj(_C_CKV), kvn_ref[...]).astype(BF16)
    kv = _dot(ckvn, wkv_ref[...])
    kr = proj(_C_KR) * cos + proj(_C_KRR) * sin
    k_out[0] = (kv[:, :nq] + jnp.tile(kr, (1, MLA_HEADS))).astype(BF16)
    v_out[0] = kv[:, nq:].astype(BF16)

    qh_out[0] = _silu(proj(_C_HQ)).astype(BF16)
    lbr = lb_ref[...]
    e = jnp.exp(lbr - jnp.max(lbr, axis=0, keepdims=True))
    lb = e[0] / jnp.sum(e, axis=0)
    lbf = lb[0:1]
    lbb = lb[1:2]
    lff_out[0] = jnp.log(lbf + (1.0 - lbf) * jax.nn.sigmoid(proj(_C_HF)))
    lfb_out[0] = jnp.log(lbb + (1.0 - lbb) * jax.nn.sigmoid(proj(_C_HB)))
    vi_out[0] = proj(_C_HI).astype(BF16)

    @pl.when(is_lat)
    def _():
        g_out[0] = _silu(proj(_C_HG)).astype(BF16)


def _inproj_call(x, ctx, mod12, nmix, win_p, qn, wq_p, kvn, wkv_p, lb_raw, cos_t, sin_t):
    B, T, D = x.shape
    L = ctx.shape[1]
    R = ROW_TILE
    n_lat = T // R
    n_all = n_lat + L // R
    TL = T + L
    nq = MLA_HEADS * HEAD_SLOT

    def lat_idx(b, i):
        return (b, jnp.minimum(i, n_lat - 1), 0)

    def all_idx(b, i):
        return (b, i, 0)

    const2 = lambda b, i: (0, 0)
    in_specs = [
        pl.BlockSpec((1, R, D), lat_idx),
        pl.BlockSpec((1, R, D), lambda b, i: (b, jnp.maximum(i - n_lat, 0), 0)),
        pl.BlockSpec((1, 1, 2, D), lambda b, i: (b, i // n_lat, 0, 0)),
        pl.BlockSpec((1, D), const2),
        pl.BlockSpec((D, IN_P_WIDTH), const2, pipeline_mode=pl.Buffered(1)),
        pl.BlockSpec((1, MLA_Q_RANK), const2),
        pl.BlockSpec((MLA_Q_RANK, 2 * nq), const2, pipeline_mode=pl.Buffered(1)),
        pl.BlockSpec((1, MLA_KV_RANK), const2),
        pl.BlockSpec((MLA_KV_RANK, 2 * nq), const2, pipeline_mode=pl.Buffered(1)),
        pl.BlockSpec(lb_raw.shape, lambda b, i: (0, 0, 0)),
        pl.BlockSpec((R, LANES), lambda b, i: (i, 0)),
        pl.BlockSpec((R, LANES), lambda b, i: (i, 0)),
    ]
    out_shape = [
        jax.ShapeDtypeStruct((B, T, nq), BF16),
        jax.ShapeDtypeStruct((B, TL, nq), BF16),
        jax.ShapeDtypeStruct((B, TL, nq), BF16),
        jax.ShapeDtypeStruct((B, TL, HG_KW), BF16),
        jax.ShapeDtypeStruct((B, TL, HG_KW), F32),
        jax.ShapeDtypeStruct((B, TL, HG_KW), F32),
        jax.ShapeDtypeStruct((B, TL, HG_KW), BF16),
        jax.ShapeDtypeStruct((B, T, HG_KW), BF16),
    ]
    out_specs = [
        pl.BlockSpec((1, R, nq), lat_idx),
        pl.BlockSpec((1, R, nq), all_idx),
        pl.BlockSpec((1, R, nq), all_idx),
        pl.BlockSpec((1, R, HG_KW), all_idx),
        pl.BlockSpec((1, R, HG_KW), all_idx),
        pl.BlockSpec((1, R, HG_KW), all_idx),
        pl.BlockSpec((1, R, HG_KW), all_idx),
        pl.BlockSpec((1, R, HG_KW), lat_idx),
    ]
    return pl.pallas_call(
        functools.partial(_inproj_kernel, n_lat),
        grid=(B, n_all),
        in_specs=in_specs,
        out_specs=out_specs,
        out_shape=out_shape,
        compiler_params=pltpu.CompilerParams(dimension_semantics=("arbitrary", "arbitrary"),
                                             vmem_limit_bytes=VMEM_LIMIT),
        name="inproj",
    )(x, ctx, mod12, nmix, win_p, qn, wq_p, kvn, wkv_p, lb_raw, cos_t, sin_t)


N_LEVELS = 8


def _replace_bit(y, row, bit, value):
    step = 1 << bit
    n = y.shape[0]
    has = ((row >> bit) & 1) == 1
    if value == 1:
        return jnp.where(has, y, pltpu.roll(y, n - step, 0))
    return jnp.where(has, pltpu.roll(y, step, 0), y)


def _hgrn_kernel(reverse, qh_ref, lf_ref, vi_ref, tri_ref, lv_ref, *rest):
    if reverse:
        of_ref, g_ref, hgn_ref, out_ref, st_ref = rest
    else:
        out_ref, st_ref = rest
    i = pl.program_id(1)

    @pl.when(i == 0)
    def _():
        st_ref[...] = jnp.zeros_like(st_ref)

    lf = lf_ref[0]
    l_hi = lf.astype(BF16)
    r1 = lf - l_hi.astype(F32)
    l_mid = r1.astype(BF16)
    l_lo = (r1 - l_mid.astype(F32)).astype(BF16)
    tri = tri_ref[...]
    b_all = _dot(tri, l_hi) + _dot(tri, l_mid) + _dot(tri, l_lo)
    k_all = 1.0 - jnp.exp(lf)
    q_all = qh_ref[0].astype(F32)
    lv = lv_ref[...]
    R = lf.shape[0]
    row = lax.broadcasted_iota(jnp.int32, (R, LANES), 0)
    near, far = (0, 1) if not reverse else (1, 0)

    for h in range(HG_HEADS):
        sl = slice(h * HG_DK, (h + 1) * HG_DK)
        bh = b_all[:, sl]
        qh = q_all[:, sl]
        kh = k_all[:, sl]
        vh = vi_ref[0, :, sl]
        a = jnp.where(lv == N_LEVELS, _dot_nt(qh.astype(BF16), kh.astype(BF16)), 0.0)
        z = bh
        for lev in range(N_LEVELS):
            bmid = _replace_bit(z, row, lev, near)
            w = jnp.exp(-jnp.abs(bh - bmid))
            p = _dot_nt((qh * w).astype(BF16), (kh * w).astype(BF16))
            a = jnp.where(lv == lev, p, a)
            z = _replace_bit(z, row, lev, far)
        st = st_ref[h]
        o = _dot(a.astype(BF16), vh) + _dot_nt((qh * jnp.exp(bh)).astype(BF16), st.astype(BF16))
        k_end = (kh * jnp.exp(-jnp.abs(bh - z))).astype(BF16)
        st_ref[h] = jnp.exp(z[0:1]) * st + _dot_tn(vh, k_end)

        if reverse:
            @pl.when(i > 0)
            def _():
                tot = of_ref[0, :, sl] + o
                out_ref[0, :, sl] = (_rms(tot, hgn_ref[...]) * g_ref[0, :, sl].astype(F32)).astype(out_ref.dtype)
        else:
            @pl.when(i > 0)
            def _():
                out_ref[0, :, sl] = o


def _hgrn_call(reverse, qh, lf, vi, tri, lv, n_lat, extra=()):
    B, TL, KW = qh.shape
    R = ROW_TILE
    n_all = TL // R
    T = n_lat * R

    if reverse:
        def seq_idx(b, i):
            return (b, jnp.where(i == 0, n_all - 1, n_lat - i), 0)

        def lat_idx(b, i):
            return (b, jnp.where(i == 0, n_lat - 1, n_lat - i), 0)
    else:
        def seq_idx(b, i):
            return (b, jnp.where(i == 0, n_all - 1, i - 1), 0)

        def lat_idx(b, i):
            return (b, jnp.maximum(i - 1, 0), 0)

    const2 = lambda b, i: (0, 0)
    in_specs = [
        pl.BlockSpec((1, R, KW), seq_idx),
        pl.BlockSpec((1, R, KW), seq_idx),
        pl.BlockSpec((1, R, KW), seq_idx),
        pl.BlockSpec((R, R), const2),
        pl.BlockSpec((R, R), const2),
    ]
    if reverse:
        in_specs += [
            pl.BlockSpec((1, R, KW), lat_idx),
            pl.BlockSpec((1, R, KW), lat_idx),
            pl.BlockSpec((1, HG_DV), const2),
        ]
        out_dtype = BF16
    else:
        out_dtype = F32
    return pl.pallas_call(
        functools.partial(_hgrn_kernel, reverse),
        grid=(B, n_lat + 1),
        in_specs=in_specs,
        out_specs=pl.BlockSpec((1, R, KW), lat_idx),
        out_shape=jax.ShapeDtypeStruct((B, T, KW), out_dtype),
        scratch_shapes=[pltpu.VMEM((HG_HEADS, HG_DV, HG_DK), F32)],
        compiler_params=pltpu.CompilerParams(dimension_semantics=("arbitrary", "arbitrary"),
                                             vmem_limit_bytes=VMEM_LIMIT),
        name="hgrn_bwd" if reverse else "hgrn_fwd",
    )(qh, lf, vi, tri, lv, *extra)


def _attn_kernel(q_ref, k_ref, v_ref, o_ref):
    outs = []
    for h in range(MLA_HEADS):
        sl = slice(h * HEAD_SLOT, (h + 1) * HEAD_SLOT)
        s = _dot_nt(q_ref[0, :, sl], k_ref[0, :, sl])
        m = jnp.max(s, axis=-1, keepdims=True)
        p = jnp.exp(s - m)
        l = jnp.sum(p, axis=-1, keepdims=True)
        o = _dot(p.astype(BF16), v_ref[0, :, sl])
        outs.append(o * (1.0 / l))
    pairs = [outs[2 * j] + pltpu.roll(outs[2 * j + 1], MLA_V, 1) for j in range(MLA_HEADS // 2)]
    o_ref[0] = jnp.concatenate(pairs, axis=-1).astype(o_ref.dtype)


def _attn_call(q, k, v):
    B, T, nq = q.shape
    TL = k.shape[1]
    R = ROW_TILE
    return pl.pallas_call(
        _attn_kernel,
        grid=(B, T // R),
        in_specs=[
            pl.BlockSpec((1, R, nq), lambda b, i: (b, i, 0)),
            pl.BlockSpec((1, TL, nq), lambda b, i: (b, 0, 0)),
            pl.BlockSpec((1, TL, nq), lambda b, i: (b, 0, 0)),
        ],
        out_specs=pl.BlockSpec((1, R, MLA_HEADS * MLA_V), lambda b, i: (b, i, 0)),
        out_shape=jax.ShapeDtypeStruct((B, T, MLA_HEADS * MLA_V), BF16),
        compiler_params=pltpu.CompilerParams(dimension_semantics=("arbitrary", "arbitrary"),
                                             vmem_limit_bytes=VMEM_LIMIT),
        name="mla_attn",
    )(q, k, v)


def _mlp_kernel(n_ff_chunks, x_ref, om_ref, oh_ref, mod_ref, woa_ref, wob_ref, nmlp_ref, w1_ref, w2_ref,
                fn_ref, o_ref):
    mod = mod_ref[0]
    y = _dot(om_ref[0], woa_ref[...]) + _dot(oh_ref[0], wob_ref[...])
    x1 = x_ref[0] + mod[0:1] * y
    h2 = (_rms(x1, nmlp_ref[...]) * (1.0 + mod[2:3]) + mod[1:2]).astype(BF16)
    ff = w1_ref.shape[1] // n_ff_chunks
    acc = jnp.zeros(x1.shape, F32)
    for c in range(n_ff_chunks):
        u = jnp.maximum(_dot(h2, w1_ref[:, c * ff:(c + 1) * ff]), 0.0)
        acc = acc + _dot((u * u).astype(BF16), w2_ref[c * ff:(c + 1) * ff, :])
    x2 = x1 + mod[3:4] * acc
    o_ref[0] = _rms(x2, fn_ref[...])


def _mlp_call(x, o_mla, o_hg, mod4, woa, wob, nmlp, w1, w2, fnorm):
    B, T, D = x.shape
    R = MLP_ROW_TILE
    dff = w1.shape[1]
    half = o_mla.shape[2]
    row_idx = lambda b, i: (b, i, 0)
    const2 = lambda b, i: (0, 0)
    return pl.pallas_call(
        functools.partial(_mlp_kernel, 4),
        grid=(B, T // R),
        in_specs=[
            pl.BlockSpec((1, R, D), row_idx),
            pl.BlockSpec((1, R, half), row_idx),
            pl.BlockSpec((1, R, half), row_idx),
            pl.BlockSpec((1, 4, D), lambda b, i: (b, 0, 0)),
            pl.BlockSpec((half, D), const2, pipeline_mode=pl.Buffered(1)),
            pl.BlockSpec((half, D), const2, pipeline_mode=pl.Buffered(1)),
            pl.BlockSpec((1, D), const2),
            pl.BlockSpec((D, dff), const2, pipeline_mode=pl.Buffered(1)),
            pl.BlockSpec((dff, D), const2, pipeline_mode=pl.Buffered(1)),
            pl.BlockSpec((1, D), const2),
        ],
        out_specs=pl.BlockSpec((1, R, D), row_idx),
        out_shape=jax.ShapeDtypeStruct((B, T, D), F32),
        compiler_params=pltpu.CompilerParams(dimension_semantics=("arbitrary", "arbitrary"),
                                             vmem_limit_bytes=VMEM_LIMIT),
        name="out_mlp",
    )(x, o_mla, o_hg, mod4, woa, wob, nmlp, w1, w2, fnorm)


def _rot_cols(w):
    a, b, c, d = w[..., 0:8], w[..., 8:16], w[..., 16:24], w[..., 24:32]
    return jnp.concatenate([-b, a, -d, c], axis=-1)


def _rope_slot(w):
    z = jnp.zeros(w.shape[:-1] + (MLA_NOPE,), w.dtype)
    z2 = jnp.zeros(w.shape[:-1] + (HEAD_SLOT - MLA_NOPE - MLA_ROPE,), w.dtype)
    return jnp.concatenate([z, w, z2], axis=-1)


def _pack_w_in(w_in):
    o = 0
    parts = []
    for n in (MLA_Q_RANK, MLA_KV_RANK, MLA_ROPE, HG_KW, HG_KW, HG_KW, HG_KW, HG_KW):
        parts.append(w_in[:, o:o + n])
        o += n
    cq, ckv, kr, hq, hf, hb, hi, hg = parts
    return jnp.concatenate([cq, ckv, _rope_slot(kr), _rope_slot(_rot_cols(kr)), hq, hf, hb, hi, hg],
                           axis=1).astype(BF16)


def _pack_w_uq(w_uq):
    k = w_uq.shape[0]
    w3 = w_uq.reshape(k, MLA_HEADS, MLA_NOPE + MLA_ROPE)
    pad = jnp.zeros((k, MLA_HEADS, HEAD_SLOT - MLA_NOPE - MLA_ROPE), w_uq.dtype)
    plain = jnp.concatenate([w3, pad], axis=-1).reshape(k, MLA_HEADS * HEAD_SLOT)
    rot = _rope_slot(_rot_cols(w3[..., MLA_NOPE:])).reshape(k, MLA_HEADS * HEAD_SLOT)
    return jnp.concatenate([plain, rot], axis=1).astype(BF16)


def _pack_w_ukv(w_ukv):
    k = w_ukv.shape[0]
    w3 = w_ukv.reshape(k, MLA_HEADS, MLA_NOPE + MLA_V)
    padk = jnp.zeros((k, MLA_HEADS, HEAD_SLOT - MLA_NOPE), w_ukv.dtype)
    padv = jnp.zeros((k, MLA_HEADS, HEAD_SLOT - MLA_V), w_ukv.dtype)
    kk = jnp.concatenate([w3[..., :MLA_NOPE], padk], axis=-1).reshape(k, MLA_HEADS * HEAD_SLOT)
    vv = jnp.concatenate([w3[..., MLA_NOPE:], padv], axis=-1).reshape(k, MLA_HEADS * HEAD_SLOT)
    return jnp.concatenate([kk, vv], axis=1).astype(BF16)


def _rope_tables(T, L):
    rows = T // GRID_W
    row = jnp.repeat(jnp.arange(rows), GRID_W).astype(F32)
    col = jnp.tile(jnp.arange(GRID_W), rows).astype(F32)
    n = MLA_ROPE // 4
    inv = ROPE_BASE ** (-jnp.arange(n, dtype=F32) / n)
    ar, ac = row[:, None] * inv, col[:, None] * inv
    cos32 = jnp.concatenate([jnp.cos(ar), jnp.cos(ar), jnp.cos(ac), jnp.cos(ac)], axis=-1)
    sin32 = jnp.concatenate([jnp.sin(ar), jnp.sin(ar), jnp.sin(ac), jnp.sin(ac)], axis=-1)
    pad_hi = HEAD_SLOT - MLA_NOPE - MLA_ROPE
    cos_t = jnp.concatenate([jnp.ones((T, MLA_NOPE), F32), cos32, jnp.ones((T, pad_hi), F32)], axis=-1)
    sin_t = jnp.concatenate([jnp.zeros((T, MLA_NOPE), F32), sin32, jnp.zeros((T, pad_hi), F32)], axis=-1)
    cos_t = jnp.concatenate([cos_t, jnp.ones((L, HEAD_SLOT), F32)], axis=0)
    sin_t = jnp.concatenate([sin_t, jnp.zeros((L, HEAD_SLOT), F32)], axis=0)
    return cos_t, sin_t


def _scan_constants(reverse):
    R = ROW_TILE
    t = jnp.arange(R, dtype=jnp.int32)[:, None]
    s = jnp.arange(R, dtype=jnp.int32)[None, :]
    if reverse:
        t, s = s, t
    x = t ^ s
    lev = jnp.zeros((R, R), jnp.int32)
    for bit in range(1, N_LEVELS):
        lev = jnp.where((x >> bit) > 0, bit, lev)
    lv = jnp.where(t > s, lev, jnp.where(t == s, N_LEVELS, -1)).astype(jnp.int32)
    tri = (t >= s).astype(BF16)
    return tri, lv


def kernel(x, c, ctx, c_ctx, w_ada, b_ada, norm_mix, w_in, q_norm, w_uq, kv_norm, w_ukv, hgrn_lb, hgrn_norm,
           w_out, norm_mlp, w_mlp_in, w_mlp_out, final_norm):
    B, T, D = x.shape
    L = ctx.shape[1]
    assert w_ada.shape[0] == 1, "single-layer block"
    assert T % ROW_TILE == 0 and L == ROW_TILE and T % MLP_ROW_TILE == 0 and T % GRID_W == 0
    n_lat = T // ROW_TILE

    pad_rows = (-(B + 1)) % 8
    cc = jnp.concatenate([c, c_ctx[None, :], jnp.zeros((pad_rows, D), F32)], axis=0)
    mod = _ada_call(cc, w_ada[0], b_ada[0][None, :])
    mod_lat = mod[:B].reshape(B, 6, D)
    mod_ctx = jnp.broadcast_to(mod[B].reshape(1, 6, D), (B, 6, D))
    mod12 = jnp.stack([mod_lat[:, 0:2], mod_ctx[:, 0:2]], axis=1)
    mod4 = mod_lat[:, 2:6]

    cos_t, sin_t = _rope_tables(T, L)
    q, k, v, qh, lff, lfb, vi, g = _inproj_call(
        x, ctx, mod12, norm_mix, _pack_w_in(w_in[0]), q_norm, _pack_w_uq(w_uq[0]), kv_norm,
        _pack_w_ukv(w_ukv[0]), hgrn_lb, cos_t, sin_t)

    tri_f, lv_f = _scan_constants(False)
    tri_b, lv_b = _scan_constants(True)
    o_f = _hgrn_call(False, qh, lff, vi, tri_f, lv_f, n_lat)
    o_hg = _hgrn_call(True, qh, lfb, vi, tri_b, lv_b, n_lat, extra=(o_f, g, hgrn_norm))

    o_mla = _attn_call(q, k, v)

    half = MLA_HEADS * MLA_V
    wo = w_out[0].astype(BF16)
    return _mlp_call(x, o_mla, o_hg, mod4, wo[:half], wo[half:], norm_mlp, w_mlp_in[0].astype(BF16),
                     w_mlp_out[0].astype(BF16), final_norm[None, :])
```

```python
import functools

import jax
import jax.numpy as jnp
from jax import lax
from jax.experimental import pallas as pl
from jax.experimental.pallas import tpu as pltpu

F32 = jnp.float32
BF16 = jnp.bfloat16

GRID_W = 64
MLA_HEADS = 8
MLA_Q_RANK = 256
MLA_KV_RANK = 128
MLA_NOPE = 64
MLA_ROPE = 32
MLA_V = 64
HG_HEADS = 4
HG_DK = 128
HG_DV = 128
HG_KW = HG_HEADS * HG_DK
ROPE_BASE = 10000.0
EPS = 1e-6

LANES = 128
HEAD_SLOT = LANES
ROW_TILE = 256
MLP_ROW_TILE = 512
ATTN_ROW_TILE = 512
VMEM_LIMIT = 56 * 1024 * 1024

_C_CQ = (0, 256)
_C_CKV = (256, 384)
_C_KR = (384, 512)
_C_KRR = (512, 640)
_C_HQ = (640, 1152)
_C_HF = (1152, 1664)
_C_HB = (1664, 2176)
_C_HI = (2176, 2688)
_C_HG = (2688, 3200)
IN_P_WIDTH = 3200


def _rms(x, g):
    ms = jnp.mean(x * x, axis=-1, keepdims=True)
    return (x * lax.rsqrt(ms + EPS)) * g


def _silu(x):
    return x * jax.nn.sigmoid(x)


def _dot(a, b):
    return jnp.dot(a, b, preferred_element_type=F32)


def _dot_nt(a, b):
    return lax.dot_general(a, b, (((1,), (1,)), ((), ())), preferred_element_type=F32)


def _dot_tn(a, b):
    return lax.dot_general(a, b, (((0,), (0,)), ((), ())), preferred_element_type=F32)


def _ada_kernel(c_ref, w_ref, b_ref, o_ref):
    s = _silu(c_ref[...])
    o_ref[...] = jnp.dot(s, w_ref[...], preferred_element_type=F32,
                         precision=lax.Precision.HIGHEST) + b_ref[...]


def _ada_call(cc, w_ada, b_ada):
    rows, d = cc.shape
    n = w_ada.shape[1]
    tn = 1024
    return pl.pallas_call(
        _ada_kernel,
        grid=(n // tn,),
        in_specs=[
            pl.BlockSpec((rows, d), lambda j: (0, 0)),
            pl.BlockSpec((d, tn), lambda j: (0, j)),
            pl.BlockSpec((1, tn), lambda j: (0, j)),
        ],
        out_specs=pl.BlockSpec((rows, tn), lambda j: (0, j)),
        out_shape=jax.ShapeDtypeStruct((rows, n), F32),
        compiler_params=pltpu.CompilerParams(dimension_semantics=("arbitrary",), vmem_limit_bytes=VMEM_LIMIT),
        name="ada_mod",
    )(cc, w_ada, b_ada)


def _inproj_kernel(n_lat, x_ref, ctx_ref, mod_ref, nmix_ref, win_ref, qn_ref, wq_ref, kvn_ref, wkv_ref,
                   lb_ref, cos_ref, sin_ref,
                   q_out, k_out, v_out, qh_out, lff_out, lfb_out, vi_out, g_out):
    i = pl.program_id(1)
    is_lat = i < n_lat
    xin = jnp.where(is_lat, x_ref[0], ctx_ref[0])
    mod = mod_ref[0, 0]
    h = _rms(xin, nmix_ref[...]) * (1.0 + mod[1:2]) + mod[0:1]
    hb = h.astype(BF16)

    def proj(c):
        return _dot(hb, win_ref[:, c[0]:c[1]])

    cos = cos_ref[...]
    sin = sin_ref[...]
    cos8 = jnp.tile(cos, (1, MLA_HEADS))
    sin8 = jnp.tile(sin, (1, MLA_HEADS))
    nq = MLA_HEADS * HEAD_SLOT

    cqn = _rms(proj(_C_CQ), qn_ref[...]).astype(BF16)
    qq = _dot(cqn, wq_ref[...])
    scale = (MLA_NOPE + MLA_ROPE) ** -0.5
    q = (qq[:, :nq] * cos8 + qq[:, nq:] * sin8) * scale

    @pl.when(is_lat)
    def _():
        q_out[0] = q.astype(BF16)

    ckvn = _rms(proj(_C_CKV), kvn_ref[...]).astype(BF16)
    kv = _dot(ckvn, wkv_ref[...])
    kr = proj(_C_KR) * cos + proj(_C_KRR) * sin
    k_out[0] = (kv[:, :nq] + jnp.tile(kr, (1, MLA_HEADS))).astype(BF16)
    v_out[0] = kv[:, nq:].astype(BF16)

    qh_out[0] = _silu(proj(_C_HQ)).astype(BF16)
    lbr = lb_ref[...]
    e = jnp.exp(lbr - jnp.max(lbr, axis=0, keepdims=True))
    lb = e[0] / jnp.sum(e, axis=0)
    lbf = lb[0:1]
    lbb = lb[1:2]
    lff_out[0] = jnp.log(lbf + (1.0 - lbf) * jax.nn.sigmoid(proj(_C_HF)))
    lfb_out[0] = jnp.log(lbb + (1.0 - lbb) * jax.nn.sigmoid(proj(_C_HB)))
    vi_out[0] = proj(_C_HI).astype(BF16)

    @pl.when(is_lat)
    def _():
        g_out[0] = _silu(proj(_C_HG)).astype(BF16)


def _inproj_call(x, ctx, mod12, nmix, win_p, qn, wq_p, kvn, wkv_p, lb_raw, cos_t, sin_t):
    B, T, D = x.shape
    L = ctx.shape[1]
    R = ROW_TILE
    n_lat = T // R
    n_all = n_lat + L // R
    TL = T + L
    nq = MLA_HEADS * HEAD_SLOT

    def lat_idx(b, i):
        return (b, jnp.minimum(i, n_lat - 1), 0)

    def all_idx(b, i):
        return (b, i, 0)

    const2 = lambda b, i: (0, 0)
    in_specs = [
        pl.BlockSpec((1, R, D), lat_idx),
        pl.BlockSpec((1, R, D), lambda b, i: (b, jnp.maximum(i - n_lat, 0), 0)),
        pl.BlockSpec((1, 1, 2, D), lambda b, i: (b, i // n_lat, 0, 0)),
        pl.BlockSpec((1, D), const2),
        pl.BlockSpec((D, IN_P_WIDTH), const2, pipeline_mode=pl.Buffered(1)),
        pl.BlockSpec((1, MLA_Q_RANK), const2),
        pl.BlockSpec((MLA_Q_RANK, 2 * nq), const2, pipeline_mode=pl.Buffered(1)),
        pl.BlockSpec((1, MLA_KV_RANK), const2),
        pl.BlockSpec((MLA_KV_RANK, 2 * nq), const2, pipeline_mode=pl.Buffered(1)),
        pl.BlockSpec(lb_raw.shape, lambda b, i: (0, 0, 0)),
        pl.BlockSpec((R, LANES), lambda b, i: (i, 0)),
        pl.BlockSpec((R, LANES), lambda b, i: (i, 0)),
    ]
    out_shape = [
        jax.ShapeDtypeStruct((B, T, nq), BF16),
        jax.ShapeDtypeStruct((B, TL, nq), BF16),
        jax.ShapeDtypeStruct((B, TL, nq), BF16),
        jax.ShapeDtypeStruct((B, TL, HG_KW), BF16),
        jax.ShapeDtypeStruct((B, TL, HG_KW), F32),
        jax.ShapeDtypeStruct((B, TL, HG_KW), F32),
        jax.ShapeDtypeStruct((B, TL, HG_KW), BF16),
        jax.ShapeDtypeStruct((B, T, HG_KW), BF16),
    ]
    out_specs = [
        pl.BlockSpec((1, R, nq), lat_idx),
        pl.BlockSpec((1, R, nq), all_idx),
        pl.BlockSpec((1, R, nq), all_idx),
        pl.BlockSpec((1, R, HG_KW), all_idx),
        pl.BlockSpec((1, R, HG_KW), all_idx),
        pl.BlockSpec((1, R, HG_KW), all_idx),
        pl.BlockSpec((1, R, HG_KW), all_idx),
        pl.BlockSpec((1, R, HG_KW), lat_idx),
    ]
    return pl.pallas_call(
        functools.partial(_inproj_kernel, n_lat),
        grid=(B, n_all),
        in_specs=in_specs,
        out_specs=out_specs,
        out_shape=out_shape,
        compiler_params=pltpu.CompilerParams(dimension_semantics=("arbitrary", "arbitrary"),
                                             vmem_limit_bytes=VMEM_LIMIT),
        name="inproj",
    )(x, ctx, mod12, nmix, win_p, qn, wq_p, kvn, wkv_p, lb_raw, cos_t, sin_t)


N_LEVELS = 8
N_SMALL_LEVELS = 3


LOG2E = 1.4426950408889634


def _neg_abs(x):
    bits = lax.bitcast_convert_type(x, jnp.uint32) | jnp.uint32(0x80000000)
    return lax.bitcast_convert_type(bits, F32)


def _replace_bit(y, row, bit, value):
    step = 1 << bit
    n = y.shape[0]
    has = ((row >> bit) & 1) == 1
    if value == 1:
        return jnp.where(has, y, pltpu.roll(y, n - step, 0))
    return jnp.where(has, pltpu.roll(y, step, 0), y)


def _hgrn_kernel(reverse, qh_ref, lf_ref, vi_ref, tri_ref, lv_ref, *rest):
    if reverse:
        of_ref, g_ref, hgn_ref, out_ref, st_ref = rest
    else:
        out_ref, st_ref = rest
    i = pl.program_id(1)

    @pl.when(i == 0)
    def _():
        st_ref[...] = jnp.zeros_like(st_ref)

    lf = lf_ref[0]
    l_hi = lf.astype(BF16)
    r1 = lf - l_hi.astype(F32)
    l_mid = r1.astype(BF16)
    l_lo = (r1 - l_mid.astype(F32)).astype(BF16)
    tri = tri_ref[...]
    b_all = (_dot(tri, l_hi) + _dot(tri, l_mid) + _dot(tri, l_lo)) * LOG2E
    k_all = 1.0 - jnp.exp(lf)
    q_all = qh_ref[0].astype(F32)
    lv = lv_ref[...]
    R = lf.shape[0]
    H = R // 2
    row = lax.broadcasted_iota(jnp.int32, (R, LANES), 0)
    near, far = (0, 1) if not reverse else (1, 0)
    end_row = 0 if reverse else R - 1
    heads = range(HG_HEADS)
    sls = [slice(h * HG_DK, (h + 1) * HG_DK) for h in heads]
    bs = [b_all[:, sl] for sl in sls]
    qs = [q_all[:, sl] for sl in sls]
    ks = [k_all[:, sl] for sl in sls]
    vs = [vi_ref[0, :, sl] for sl in sls]

    diag = lv == N_LEVELS - 1
    a0 = [jnp.where(diag, _dot_nt(qs[h][:H].astype(BF16), ks[h][:H].astype(BF16)), 0.0) for h in heads]
    a1 = [jnp.where(diag, _dot_nt(qs[h][H:].astype(BF16), ks[h][H:].astype(BF16)), 0.0) for h in heads]
    a_off = [None] * HG_HEADS
    zs = list(bs)
    for lev in range(N_LEVELS):
        m = 1 << lev
        mask = lv == lev
        q_side = ((row >> lev) & 1) == far
        for h in heads:
            bh, qh, kh = bs[h], qs[h], ks[h]
            if lev < N_SMALL_LEVELS:
                bmid = _replace_bit(zs[h], row, lev, near)
                zs[h] = _replace_bit(zs[h], row, lev, far)
                qk = jnp.where(q_side, qh, kh)
                arg = _neg_abs(bh - bmid)
            else:
                args, parts = [], []
                for j in range(R // (2 * m)):
                    lo = j * 2 * m
                    ref = lo + (m if reverse else m - 1)
                    bref = bh[ref:ref + 1]
                    if reverse:
                        args += [bh[lo:lo + m] - bref, bref - bh[lo + m:lo + 2 * m]]
                        parts += [qh[lo:lo + m], kh[lo + m:lo + 2 * m]]
                    else:
                        args += [bref - bh[lo:lo + m], bh[lo + m:lo + 2 * m] - bref]
                        parts += [kh[lo:lo + m], qh[lo + m:lo + 2 * m]]
                arg = jnp.concatenate(args, axis=0)
                qk = jnp.concatenate(parts, axis=0)
            u = (qk * jnp.exp2(arg)).astype(BF16)
            if lev < N_LEVELS - 1:
                a0[h] = jnp.where(mask, _dot_nt(u[:H], u[:H]), a0[h])
                a1[h] = jnp.where(mask, _dot_nt(u[H:], u[H:]), a1[h])
            elif reverse:
                a_off[h] = _dot_nt(u[:H], u[H:])
            else:
                a_off[h] = _dot_nt(u[H:], u[:H])

    for h in heads:
        sl, bh, qh, kh, vh = sls[h], bs[h], qs[h], ks[h], vs[h]
        zend = bh[end_row:end_row + 1]
        st = st_ref[h]
        if reverse:
            o_in = jnp.concatenate([
                _dot(jnp.concatenate([a0[h], a_off[h]], axis=1).astype(BF16), vh),
                _dot(a1[h].astype(BF16), vh[H:])], axis=0)
        else:
            o_in = jnp.concatenate([
                _dot(a0[h].astype(BF16), vh[:H]),
                _dot(jnp.concatenate([a_off[h], a1[h]], axis=1).astype(BF16), vh)], axis=0)
        o = o_in + _dot_nt((qh * jnp.exp2(bh)).astype(BF16), st.astype(BF16))
        k_end = (kh * jnp.exp2(zend - bh)).astype(BF16)
        st_ref[h] = jnp.exp2(zend) * st + _dot_tn(vh, k_end)

        if reverse:
            @pl.when(i > 0)
            def _():
                tot = of_ref[0, :, sl] + o
                out_ref[0, :, sl] = (_rms(tot, hgn_ref[...]) * g_ref[0, :, sl].astype(F32)).astype(out_ref.dtype)
        else:
            @pl.when(i > 0)
            def _():
                out_ref[0, :, sl] = o


def _hgrn_call(reverse, qh, lf, vi, tri, lv, n_lat, extra=()):
    B, TL, KW = qh.shape
    R = ROW_TILE
    n_all = TL // R
    T = n_lat * R

    if reverse:
        def seq_idx(b, i):
            return (b, jnp.where(i == 0, n_all - 1, n_lat - i), 0)

        def lat_idx(b, i):
            return (b, jnp.where(i == 0, n_lat - 1, n_lat - i), 0)
    else:
        def seq_idx(b, i):
            return (b, jnp.where(i == 0, n_all - 1, i - 1), 0)

        def lat_idx(b, i):
            return (b, jnp.maximum(i - 1, 0), 0)

    const2 = lambda b, i: (0, 0)
    in_specs = [
        pl.BlockSpec((1, R, KW), seq_idx),
        pl.BlockSpec((1, R, KW), seq_idx),
        pl.BlockSpec((1, R, KW), seq_idx),
        pl.BlockSpec((R, R), const2),
        pl.BlockSpec((R // 2, R // 2), const2),
    ]
    if reverse:
        in_specs += [
            pl.BlockSpec((1, R, KW), lat_idx),
            pl.BlockSpec((1, R, KW), lat_idx),
            pl.BlockSpec((1, HG_DV), const2),
        ]
        out_dtype = BF16
    else:
        out_dtype = F32
    return pl.pallas_call(
        functools.partial(_hgrn_kernel, reverse),
        grid=(B, n_lat + 1),
        in_specs=in_specs,
        out_specs=pl.BlockSpec((1, R, KW), lat_idx),
        out_shape=jax.ShapeDtypeStruct((B, T, KW), out_dtype),
        scratch_shapes=[pltpu.VMEM((HG_HEADS, HG_DV, HG_DK), F32)],
        compiler_params=pltpu.CompilerParams(dimension_semantics=("arbitrary", "arbitrary"),
                                             vmem_limit_bytes=VMEM_LIMIT),
        name="hgrn_bwd" if reverse else "hgrn_fwd",
    )(qh, lf, vi, tri, lv, *extra)


def _attn_kernel(q_ref, k_ref, v_ref, o_ref):
    outs = []
    for h in range(MLA_HEADS):
        sl = slice(h * HEAD_SLOT, (h + 1) * HEAD_SLOT)
        s = _dot_nt(q_ref[0, :, sl], k_ref[0, :, sl])
        m = jnp.max(s, axis=-1, keepdims=True)
        p = jnp.exp(s - m)
        l = jnp.sum(p, axis=-1, keepdims=True)
        o = _dot(p.astype(BF16), v_ref[0, :, sl])
        outs.append(o * (1.0 / l))
    pairs = [outs[2 * j] + pltpu.roll(outs[2 * j + 1], MLA_V, 1) for j in range(MLA_HEADS // 2)]
    o_ref[0] = jnp.concatenate(pairs, axis=-1).astype(o_ref.dtype)


def _attn_call(q, k, v):
    B, T, nq = q.shape
    TL = k.shape[1]
    R = ATTN_ROW_TILE
    return pl.pallas_call(
        _attn_kernel,
        grid=(B, T // R),
        in_specs=[
            pl.BlockSpec((1, R, nq), lambda b, i: (b, i, 0)),
            pl.BlockSpec((1, TL, nq), lambda b, i: (b, 0, 0)),
            pl.BlockSpec((1, TL, nq), lambda b, i: (b, 0, 0)),
        ],
        out_specs=pl.BlockSpec((1, R, MLA_HEADS * MLA_V), lambda b, i: (b, i, 0)),
        out_shape=jax.ShapeDtypeStruct((B, T, MLA_HEADS * MLA_V), BF16),
        compiler_params=pltpu.CompilerParams(dimension_semantics=("arbitrary", "arbitrary"),
                                             vmem_limit_bytes=VMEM_LIMIT),
        name="mla_attn",
    )(q, k, v)


def _mlp_kernel(n_ff_chunks, x_ref, om_ref, oh_ref, mod_ref, woa_ref, wob_ref, nmlp_ref, w1_ref, w2_ref,
                fn_ref, o_ref):
    mod = mod_ref[0]
    y = _dot(om_ref[0], woa_ref[...]) + _dot(oh_ref[0], wob_ref[...])
    x1 = x_ref[0] + mod[0:1] * y
    h2 = (_rms(x1, nmlp_ref[...]) * (1.0 + mod[2:3]) + mod[1:2]).astype(BF16)
    ff = w1_ref.shape[1] // n_ff_chunks
    acc = jnp.zeros(x1.shape, F32)
    for c in range(n_ff_chunks):
        u = jnp.maximum(_dot(h2, w1_ref[:, c * ff:(c + 1) * ff]), 0.0)
        acc = acc + _dot((u * u).astype(BF16), w2_ref[c * ff:(c + 1) * ff, :])
    x2 = x1 + mod[3:4] * acc
    o_ref[0] = _rms(x2, fn_ref[...])


def _mlp_call(x, o_mla, o_hg, mod4, woa, wob, nmlp, w1, w2, fnorm):
    B, T, D = x.shape
    R = MLP_ROW_TILE
    dff = w1.shape[1]
    half = o_mla.shape[2]
    row_idx = lambda b, i: (b, i, 0)
    const2 = lambda b, i: (0, 0)
    return pl.pallas_call(
        functools.partial(_mlp_kernel, 4),
        grid=(B, T // R),
        in_specs=[
            pl.BlockSpec((1, R, D), row_idx),
            pl.BlockSpec((1, R, half), row_idx),
            pl.BlockSpec((1, R, half), row_idx),
            pl.BlockSpec((1, 4, D), lambda b, i: (b, 0, 0)),
            pl.BlockSpec((half, D), const2, pipeline_mode=pl.Buffered(1)),
            pl.BlockSpec((half, D), const2, pipeline_mode=pl.Buffered(1)),
            pl.BlockSpec((1, D), const2),
            pl.BlockSpec((D, dff), const2, pipeline_mode=pl.Buffered(1)),
            pl.BlockSpec((dff, D), const2, pipeline_mode=pl.Buffered(1)),
            pl.BlockSpec((1, D), const2),
        ],
        out_specs=pl.BlockSpec((1, R, D), row_idx),
        out_shape=jax.ShapeDtypeStruct((B, T, D), F32),
        compiler_params=pltpu.CompilerParams(dimension_semantics=("arbitrary", "arbitrary"),
                                             vmem_limit_bytes=VMEM_LIMIT),
        name="out_mlp",
    )(x, o_mla, o_hg, mod4, woa, wob, nmlp, w1, w2, fnorm)


def _rot_cols(w):
    a, b, c, d = w[..., 0:8], w[..., 8:16], w[..., 16:24], w[..., 24:32]
    return jnp.concatenate([-b, a, -d, c], axis=-1)


def _rope_slot(w):
    z = jnp.zeros(w.shape[:-1] + (MLA_NOPE,), w.dtype)
    z2 = jnp.zeros(w.shape[:-1] + (HEAD_SLOT - MLA_NOPE - MLA_ROPE,), w.dtype)
    return jnp.concatenate([z, w, z2], axis=-1)


def _pack_w_in(w_in):
    o = 0
    parts = []
    for n in (MLA_Q_RANK, MLA_KV_RANK, MLA_ROPE, HG_KW, HG_KW, HG_KW, HG_KW, HG_KW):
        parts.append(w_in[:, o:o + n])
        o += n
    cq, ckv, kr, hq, hf, hb, hi, hg = parts
    return jnp.concatenate([cq, ckv, _rope_slot(kr), _rope_slot(_rot_cols(kr)), hq, hf, hb, hi, hg],
                           axis=1).astype(BF16)


def _pack_w_uq(w_uq):
    k = w_uq.shape[0]
    w3 = w_uq.reshape(k, MLA_HEADS, MLA_NOPE + MLA_ROPE)
    pad = jnp.zeros((k, MLA_HEADS, HEAD_SLOT - MLA_NOPE - MLA_ROPE), w_uq.dtype)
    plain = jnp.concatenate([w3, pad], axis=-1).reshape(k, MLA_HEADS * HEAD_SLOT)
    rot = _rope_slot(_rot_cols(w3[..., MLA_NOPE:])).reshape(k, MLA_HEADS * HEAD_SLOT)
    return jnp.concatenate([plain, rot], axis=1).astype(BF16)


def _pack_w_ukv(w_ukv):
    k = w_ukv.shape[0]
    w3 = w_ukv.reshape(k, MLA_HEADS, MLA_NOPE + MLA_V)
    padk = jnp.zeros((k, MLA_HEADS, HEAD_SLOT - MLA_NOPE), w_ukv.dtype)
    padv = jnp.zeros((k, MLA_HEADS, HEAD_SLOT - MLA_V), w_ukv.dtype)
    kk = jnp.concatenate([w3[..., :MLA_NOPE], padk], axis=-1).reshape(k, MLA_HEADS * HEAD_SLOT)
    vv = jnp.concatenate([w3[..., MLA_NOPE:], padv], axis=-1).reshape(k, MLA_HEADS * HEAD_SLOT)
    return jnp.concatenate([kk, vv], axis=1).astype(BF16)


def _rope_tables(T, L):
    rows = T // GRID_W
    row = jnp.repeat(jnp.arange(rows), GRID_W).astype(F32)
    col = jnp.tile(jnp.arange(GRID_W), rows).astype(F32)
    n = MLA_ROPE // 4
    inv = ROPE_BASE ** (-jnp.arange(n, dtype=F32) / n)
    ar, ac = row[:, None] * inv, col[:, None] * inv
    cos32 = jnp.concatenate([jnp.cos(ar), jnp.cos(ar), jnp.cos(ac), jnp.cos(ac)], axis=-1)
    sin32 = jnp.concatenate([jnp.sin(ar), jnp.sin(ar), jnp.sin(ac), jnp.sin(ac)], axis=-1)
    pad_hi = HEAD_SLOT - MLA_NOPE - MLA_ROPE
    cos_t = jnp.concatenate([jnp.ones((T, MLA_NOPE), F32), cos32, jnp.ones((T, pad_hi), F32)], axis=-1)
    sin_t = jnp.concatenate([jnp.zeros((T, MLA_NOPE), F32), sin32, jnp.zeros((T, pad_hi), F32)], axis=-1)
    cos_t = jnp.concatenate([cos_t, jnp.ones((L, HEAD_SLOT), F32)], axis=0)
    sin_t = jnp.concatenate([sin_t, jnp.zeros((L, HEAD_SLOT), F32)], axis=0)
    return cos_t, sin_t


def _scan_constants(reverse):
    R = ROW_TILE
    t = jnp.arange(R, dtype=jnp.int32)[:, None]
    s = jnp.arange(R, dtype=jnp.int32)[None, :]
    if reverse:
        t, s = s, t
    tri = (t >= s).astype(BF16)
    H = R // 2
    t, s = jnp.broadcast_to(t, (R, R))[:H, :H], jnp.broadcast_to(s, (R, R))[:H, :H]
    x = t ^ s
    lev = jnp.zeros((H, H), jnp.int32)
    for bit in range(1, N_LEVELS - 1):
        lev = jnp.where((x >> bit) > 0, bit, lev)
    lv = jnp.where(t > s, lev, jnp.where(t == s, N_LEVELS - 1, -1)).astype(jnp.int32)
    return tri, lv


def kernel(x, c, ctx, c_ctx, w_ada, b_ada, norm_mix, w_in, q_norm, w_uq, kv_norm, w_ukv, hgrn_lb, hgrn_norm,
           w_out, norm_mlp, w_mlp_in, w_mlp_out, final_norm):
    B, T, D = x.shape
    L = ctx.shape[1]
    assert w_ada.shape[0] == 1, "single-layer block"
    assert T % ROW_TILE == 0 and L == ROW_TILE and T % GRID_W == 0
    assert T % MLP_ROW_TILE == 0 and T % ATTN_ROW_TILE == 0
    n_lat = T // ROW_TILE

    pad_rows = (-(B + 1)) % 8
    cc = jnp.concatenate([c, c_ctx[None, :], jnp.zeros((pad_rows, D), F32)], axis=0)
    mod = _ada_call(cc, w_ada[0], b_ada[0][None, :])
    mod_lat = mod[:B].reshape(B, 6, D)
    mod_ctx = jnp.broadcast_to(mod[B].reshape(1, 6, D), (B, 6, D))
    mod12 = jnp.stack([mod_lat[:, 0:2], mod_ctx[:, 0:2]], axis=1)
    mod4 = mod_lat[:, 2:6]

    cos_t, sin_t = _rope_tables(T, L)
    q, k, v, qh, lff, lfb, vi, g = _inproj_call(
        x, ctx, mod12, norm_mix, _pack_w_in(w_in[0]), q_norm, _pack_w_uq(w_uq[0]), kv_norm,
        _pack_w_ukv(w_ukv[0]), hgrn_lb, cos_t, sin_t)

    tri_f, lv_f = _scan_constants(False)
    tri_b, lv_b = _scan_constants(True)
    o_f = _hgrn_call(False, qh, lff, vi, tri_f, lv_f, n_lat)
    o_hg = _hgrn_call(True, qh, lfb, vi, tri_b, lv_b, n_lat, extra=(o_f, g, hgrn_norm))

    o_mla = _attn_call(q, k, v)

    half = MLA_HEADS * MLA_V
    wo = w_out[0].astype(BF16)
    return _mlp_call(x, o_mla, o_hg, mod4, wo[:half], wo[half:], norm_mlp, w_mlp_in[0].astype(BF16),
                     w_mlp_out[0].astype(BF16), final_norm[None, :])
```

```python
import functools

import jax
import jax.numpy as jnp
from jax import lax
from jax.experimental import pallas as pl
from jax.experimental.pallas import tpu as pltpu

F32 = jnp.float32
BF16 = jnp.bfloat16

GRID_W = 64
MLA_HEADS = 8
MLA_Q_RANK = 256
MLA_KV_RANK = 128
MLA_NOPE = 64
MLA_ROPE = 32
MLA_V = 64
HG_HEADS = 4
HG_DK = 128
HG_DV = 128
HG_KW = HG_HEADS * HG_DK
ROPE_BASE = 10000.0
EPS = 1e-6

LANES = 128
HEAD_SLOT = LANES
ROW_TILE = 256
MLP_ROW_TILE = 512
PROJ_ROW_TILE = 512
ATTN_ROW_TILE = 512
VMEM_LIMIT = 56 * 1024 * 1024

_C_CQ = (0, 256)
_C_CKV = (256, 384)
_C_KR = (384, 512)
_C_HQ = (512, 1024)
_C_HF = (1024, 1536)
_C_HB = (1536, 2048)
_C_HI = (2048, 2560)
_C_HG = (2560, 3072)
IN_P_WIDTH = 3072


def _rms(x, g):
    ms = jnp.mean(x * x, axis=-1, keepdims=True)
    return (x * lax.rsqrt(ms + EPS)) * g


def _silu(x):
    return x * jax.nn.sigmoid(x)


def _dot(a, b):
    return jnp.dot(a, b, preferred_element_type=F32)


def _dot_nt(a, b):
    return lax.dot_general(a, b, (((1,), (1,)), ((), ())), preferred_element_type=F32)


def _dot_tn(a, b):
    return lax.dot_general(a, b, (((0,), (0,)), ((), ())), preferred_element_type=F32)


def _ada_kernel(c_ref, w_ref, b_ref, o_ref):
    s = _silu(c_ref[...])
    o_ref[...] = jnp.dot(s, w_ref[...], preferred_element_type=F32,
                         precision=lax.Precision.HIGHEST) + b_ref[...]


def _ada_call(cc, w_ada, b_ada):
    rows, d = cc.shape
    n = w_ada.shape[1]
    tn = 1024
    return pl.pallas_call(
        _ada_kernel,
        grid=(n // tn,),
        in_specs=[
            pl.BlockSpec((rows, d), lambda j: (0, 0)),
            pl.BlockSpec((d, tn), lambda j: (0, j)),
            pl.BlockSpec((1, tn), lambda j: (0, j)),
        ],
        out_specs=pl.BlockSpec((rows, tn), lambda j: (0, j)),
        out_shape=jax.ShapeDtypeStruct((rows, n), F32),
        compiler_params=pltpu.CompilerParams(dimension_semantics=("arbitrary",), vmem_limit_bytes=VMEM_LIMIT),
        name="ada_mod",
    )(cc, w_ada, b_ada)


def _lane_lt(n_rows, bound):
    return lax.broadcasted_iota(jnp.int32, (n_rows, LANES), 1) < bound


def _rotate_half_lanes(x):
    lane = lax.broadcasted_iota(jnp.int32, x.shape, 1)
    return jnp.where((lane & 15) < 8, -pltpu.roll(x, LANES - 8, 1), pltpu.roll(x, 8, 1))


def _head_slots(nope, rope_lo, rope_hi):
    first = _lane_lt(nope.shape[0], MLA_NOPE)
    return (jnp.where(first, nope, rope_lo), jnp.where(first, pltpu.roll(nope, MLA_NOPE, 1), rope_hi))


def _inproj_kernel(is_ctx, *refs):
    if is_ctx:
        (x_ref, mod_ref, nmix_ref, win_ref, kvn_ref, wkv_ref, lb_ref, _k, _v, _qh, _lff, _lfb, _vi,
         k_out, v_out, qh_out, lff_out, lfb_out, vi_out) = refs
    else:
        (x_ref, mod_ref, nmix_ref, win_ref, qn_ref, wq_ref, kvn_ref, wkv_ref, lb_ref, cos_ref, sin_ref,
         q_out, k_out, v_out, qh_out, lff_out, lfb_out, vi_out, g_out) = refs
    mod = mod_ref[0]
    h = _rms(x_ref[0], nmix_ref[...]) * (1.0 + mod[1:2]) + mod[0:1]
    hb = h.astype(BF16)
    n_pairs = MLA_HEADS // 2
    half = n_pairs * LANES

    def proj(c):
        return _dot(hb, win_ref[:, c[0]:c[1]])

    def put(ref, val):
        ref[...] = val.astype(ref.dtype).reshape(ref.shape)

    def pair(a, j):
        return a[:, j * LANES:(j + 1) * LANES]

    lbr = lb_ref[...]
    e = jnp.exp(lbr - jnp.max(lbr, axis=0, keepdims=True))
    lb = e[0] / jnp.sum(e, axis=0)
    lbf = lb[0:1]
    lbb = lb[1:2]

    ckv = proj(_C_CKV)
    kr = proj(_C_KR)
    if not is_ctx:
        cq = proj(_C_CQ)
    put(qh_out, _silu(proj(_C_HQ)))
    kv = _dot(_rms(ckv, kvn_ref[...]).astype(BF16), wkv_ref[...])
    put(lff_out, jnp.log(lbf + (1.0 - lbf) * jax.nn.sigmoid(proj(_C_HF))))
    if not is_ctx:
        cos = cos_ref[...]
        sin = sin_ref[...]
        kr = kr * cos + _rotate_half_lanes(kr) * sin
        qq = _dot(_rms(cq, qn_ref[...]).astype(BF16), wq_ref[...])
    put(lfb_out, jnp.log(lbb + (1.0 - lbb) * jax.nn.sigmoid(proj(_C_HB))))
    k_slots = []
    for j in range(n_pairs):
        k_slots += _head_slots(pair(kv, j), kr, kr)
    put(k_out, jnp.concatenate(k_slots, axis=1))
    put(v_out, kv[:, half:])
    put(vi_out, proj(_C_HI))
    if not is_ctx:
        scale = (MLA_NOPE + MLA_ROPE) ** -0.5
        q_slots = []
        for j in range(n_pairs):
            r = pair(qq, n_pairs + j)
            r = (r * cos + _rotate_half_lanes(r) * sin) * scale
            q_slots += _head_slots(pair(qq, j) * scale, r, pltpu.roll(r, LANES - MLA_ROPE, 1))
        put(q_out, jnp.concatenate(q_slots, axis=1))
        put(g_out, _silu(proj(_C_HG)))


def _inproj_lat_call(x, mod_lat, nmix, win_p, qn, wq_p, kvn, wkv_p, lb_raw, cos_t, sin_t, L):
    B, T, D = x.shape
    R = PROJ_ROW_TILE
    TL = T + L
    nq = MLA_HEADS * HEAD_SLOT
    nv = MLA_HEADS * MLA_V
    row_idx = lambda b, i: (b, i, 0)
    const2 = lambda b, i: (0, 0)
    in_specs = [
        pl.BlockSpec((1, R, D), row_idx),
        pl.BlockSpec((1, 2, D), lambda b, i: (b, 0, 0)),
        pl.BlockSpec((1, D), const2),
        pl.BlockSpec((D, IN_P_WIDTH), const2, pipeline_mode=pl.Buffered(1)),
        pl.BlockSpec((1, MLA_Q_RANK), const2),
        pl.BlockSpec((MLA_Q_RANK, nq), const2, pipeline_mode=pl.Buffered(1)),
        pl.BlockSpec((1, MLA_KV_RANK), const2),
        pl.BlockSpec((MLA_KV_RANK, nq), const2, pipeline_mode=pl.Buffered(1)),
        pl.BlockSpec(lb_raw.shape, lambda b, i: (0, 0, 0)),
        pl.BlockSpec((R, LANES), lambda b, i: (i, 0)),
        pl.BlockSpec((R, LANES), lambda b, i: (i, 0)),
    ]
    out_shape = [
        jax.ShapeDtypeStruct((B, T, nq), BF16),
        jax.ShapeDtypeStruct((B, TL, nq), BF16),
        jax.ShapeDtypeStruct((B, TL, nv), BF16),
        jax.ShapeDtypeStruct((B, TL, HG_KW), BF16),
        jax.ShapeDtypeStruct((B, TL, HG_KW), F32),
        jax.ShapeDtypeStruct((B, TL, HG_KW), F32),
        jax.ShapeDtypeStruct((B, TL, HG_KW), BF16),
        jax.ShapeDtypeStruct((B, T, HG_KW), BF16),
    ]
    out_specs = ([pl.BlockSpec((1, R, nq), row_idx)] * 2 + [pl.BlockSpec((1, R, nv), row_idx)]
                 + [pl.BlockSpec((1, R, HG_KW), row_idx)] * 5)
    return pl.pallas_call(
        functools.partial(_inproj_kernel, False),
        grid=(B, T // R),
        in_specs=in_specs,
        out_specs=out_specs,
        out_shape=out_shape,
        compiler_params=pltpu.CompilerParams(dimension_semantics=("arbitrary", "arbitrary"),
                                             vmem_limit_bytes=VMEM_LIMIT),
        name="inproj_lat",
    )(x, mod_lat, nmix, win_p, qn, wq_p, kvn, wkv_p, lb_raw, cos_t, sin_t)


def _inproj_ctx_call(ctx, mod_ctx, nmix, win_p, kvn, wkv_p, lb_raw, dests, T):
    B, L, D = ctx.shape
    R = PROJ_ROW_TILE
    per = R // L
    nq = MLA_HEADS * HEAD_SLOT
    x2 = ctx.reshape(B // per, R, D)
    const2 = lambda i: (0, 0)
    in_specs = [
        pl.BlockSpec((1, R, D), lambda i: (i, 0, 0)),
        pl.BlockSpec((1, 2, D), lambda i: (0, 0, 0)),
        pl.BlockSpec((1, D), const2),
        pl.BlockSpec((D, IN_P_WIDTH), const2, pipeline_mode=pl.Buffered(1)),
        pl.BlockSpec((1, MLA_KV_RANK), const2),
        pl.BlockSpec((MLA_KV_RANK, nq), const2, pipeline_mode=pl.Buffered(1)),
        pl.BlockSpec(lb_raw.shape, lambda i: (0, 0, 0)),
    ] + [pl.BlockSpec(memory_space=pl.ANY)] * len(dests)
    n_fixed = len(in_specs) - len(dests)
    ctx_blk = lambda i: (i, T // L, 0)
    out_specs = [pl.BlockSpec((per, L, d.shape[2]), ctx_blk) for d in dests]
    return pl.pallas_call(
        functools.partial(_inproj_kernel, True),
        grid=(B // per,),
        in_specs=in_specs,
        out_specs=out_specs,
        out_shape=[jax.ShapeDtypeStruct(d.shape, d.dtype) for d in dests],
        input_output_aliases={n_fixed + j: j for j in range(len(dests))},
        compiler_params=pltpu.CompilerParams(dimension_semantics=("arbitrary",), vmem_limit_bytes=VMEM_LIMIT),
        name="inproj_ctx",
    )(x2, mod_ctx, nmix, win_p, kvn, wkv_p, lb_raw, *dests)


N_LEVELS = 8
N_SMALL_LEVELS = 3
LOG2E = 1.4426950408889634


def _neg_abs(x):
    bits = lax.bitcast_convert_type(x, jnp.uint32) | jnp.uint32(0x80000000)
    return lax.bitcast_convert_type(bits, F32)


def _replace_bit(y, row, bit, value):
    step = 1 << bit
    n = y.shape[0]
    has = ((row >> bit) & 1) == 1
    if value == 1:
        return jnp.where(has, y, pltpu.roll(y, n - step, 0))
    return jnp.where(has, pltpu.roll(y, step, 0), y)


def _hgrn_kernel(reverse, qh_ref, lf_ref, vi_ref, tri_ref, lv_ref, *rest):
    if reverse:
        of_ref, g_ref, hgn_ref, out_ref, st_ref = rest
    else:
        out_ref, st_ref = rest
    i = pl.program_id(1)

    @pl.when(i == 0)
    def _():
        st_ref[...] = jnp.zeros_like(st_ref)

    lf = lf_ref[0]
    l_hi = lf.astype(BF16)
    r1 = lf - l_hi.astype(F32)
    l_mid = r1.astype(BF16)
    l_lo = (r1 - l_mid.astype(F32)).astype(BF16)
    tri = tri_ref[...]
    b_all = (_dot(tri, l_hi) + _dot(tri, l_mid) + _dot(tri, l_lo)) * LOG2E
    k_all = 1.0 - jnp.exp(lf)
    q_all = qh_ref[0].astype(F32)
    lv = lv_ref[...]
    R = lf.shape[0]
    H = R // 2
    row = lax.broadcasted_iota(jnp.int32, (R, LANES), 0)
    near, far = (0, 1) if not reverse else (1, 0)
    end_row = 0 if reverse else R - 1
    heads = range(HG_HEADS)
    sls = [slice(h * HG_DK, (h + 1) * HG_DK) for h in heads]
    bs = [b_all[:, sl] for sl in sls]
    qs = [q_all[:, sl] for sl in sls]
    ks = [k_all[:, sl] for sl in sls]
    vs = [vi_ref[0, :, sl] for sl in sls]

    diag = lv == N_LEVELS - 1
    a0 = [jnp.where(diag, _dot_nt(qs[h][:H].astype(BF16), ks[h][:H].astype(BF16)), 0.0) for h in heads]
    a1 = [jnp.where(diag, _dot_nt(qs[h][H:].astype(BF16), ks[h][H:].astype(BF16)), 0.0) for h in heads]
    a_off = [None] * HG_HEADS
    zs = list(bs)
    for lev in range(N_LEVELS):
        m = 1 << lev
        mask = lv == lev
        q_side = ((row >> lev) & 1) == far
        for h in heads:
            bh, qh, kh = bs[h], qs[h], ks[h]
            if lev < N_SMALL_LEVELS:
                bmid = _replace_bit(zs[h], row, lev, near)
                zs[h] = _replace_bit(zs[h], row, lev, far)
                qk = jnp.where(q_side, qh, kh)
                arg = _neg_abs(bh - bmid)
            else:
                args, parts = [], []
                for j in range(R // (2 * m)):
                    lo = j * 2 * m
                    ref = lo + (m if reverse else m - 1)
                    bref = bh[ref:ref + 1]
                    if reverse:
                        args += [bh[lo:lo + m] - bref, bref - bh[lo + m:lo + 2 * m]]
                        parts += [qh[lo:lo + m], kh[lo + m:lo + 2 * m]]
                    else:
                        args += [bref - bh[lo:lo + m], bh[lo + m:lo + 2 * m] - bref]
                        parts += [kh[lo:lo + m], qh[lo + m:lo + 2 * m]]
                arg = jnp.concatenate(args, axis=0)
                qk = jnp.concatenate(parts, axis=0)
            u = (qk * jnp.exp2(arg)).astype(BF16)
            if lev < N_LEVELS - 1:
                a0[h] = jnp.where(mask, _dot_nt(u[:H], u[:H]), a0[h])
                a1[h] = jnp.where(mask, _dot_nt(u[H:], u[H:]), a1[h])
            elif reverse:
                a_off[h] = _dot_nt(u[:H], u[H:])
            else:
                a_off[h] = _dot_nt(u[H:], u[:H])

    for h in heads:
        sl, bh, qh, kh, vh = sls[h], bs[h], qs[h], ks[h], vs[h]
        zend = bh[end_row:end_row + 1]
        st = st_ref[h]
        if reverse:
            o_in = jnp.concatenate([
                _dot(jnp.concatenate([a0[h], a_off[h]], axis=1).astype(BF16), vh),
                _dot(a1[h].astype(BF16), vh[H:])], axis=0)
        else:
            o_in = jnp.concatenate([
                _dot(a0[h].astype(BF16), vh[:H]),
                _dot(jnp.concatenate([a_off[h], a1[h]], axis=1).astype(BF16), vh)], axis=0)
        o = o_in + _dot_nt((qh * jnp.exp2(bh)).astype(BF16), st.astype(BF16))
        k_end = (kh * jnp.exp2(zend - bh)).astype(BF16)
        st_ref[h] = jnp.exp2(zend) * st + _dot_tn(vh, k_end)

        if reverse:
            tot = of_ref[0, :, sl] + o
            out_ref[0, :, sl] = (_rms(tot, hgn_ref[...]) * g_ref[0, :, sl].astype(F32)).astype(out_ref.dtype)
        else:
            out_ref[0, :, sl] = o


def _hgrn_call(reverse, qh, lf, vi, tri, lv, n_lat, extra=()):
    B, TL, KW = qh.shape
    R = ROW_TILE
    n_all = TL // R
    T = n_lat * R

    if reverse:
        def seq_idx(b, i):
            return (b, jnp.where(i == 0, n_all - 1, n_lat - i), 0)

        def lat_idx(b, i):
            return (b, jnp.where(i == 0, n_lat - 1, n_lat - i), 0)
    else:
        def seq_idx(b, i):
            return (b, jnp.where(i == 0, n_all - 1, i - 1), 0)

        def lat_idx(b, i):
            return (b, jnp.maximum(i - 1, 0), 0)

    const2 = lambda b, i: (0, 0)
    in_specs = [
        pl.BlockSpec((1, R, KW), seq_idx),
        pl.BlockSpec((1, R, KW), seq_idx),
        pl.BlockSpec((1, R, KW), seq_idx),
        pl.BlockSpec((R, R), const2),
        pl.BlockSpec((R // 2, R // 2), const2),
    ]
    if reverse:
        in_specs += [
            pl.BlockSpec((1, R, KW), lat_idx),
            pl.BlockSpec((1, R, KW), lat_idx),
            pl.BlockSpec((1, HG_DV), const2),
        ]
        out_dtype = BF16
    else:
        out_dtype = F32
    return pl.pallas_call(
        functools.partial(_hgrn_kernel, reverse),
        grid=(B, n_lat + 1),
        in_specs=in_specs,
        out_specs=pl.BlockSpec((1, R, KW), lat_idx),
        out_shape=jax.ShapeDtypeStruct((B, T, KW), out_dtype),
        scratch_shapes=[pltpu.VMEM((HG_HEADS, HG_DV, HG_DK), F32)],
        compiler_params=pltpu.CompilerParams(dimension_semantics=("arbitrary", "arbitrary"),
                                             vmem_limit_bytes=VMEM_LIMIT),
        name="hgrn_bwd" if reverse else "hgrn_fwd",
    )(qh, lf, vi, tri, lv, *extra)


def _attn_kernel(q_ref, k_ref, v_ref, o_ref):
    first = _lane_lt(q_ref.shape[1], MLA_V)
    pairs = []
    for j in range(MLA_HEADS // 2):
        v_pair = v_ref[0, :, j * LANES:(j + 1) * LANES]
        outs = []
        for h in (2 * j, 2 * j + 1):
            sl = slice(h * HEAD_SLOT, (h + 1) * HEAD_SLOT)
            s = _dot_nt(q_ref[0, :, sl], k_ref[0, :, sl])
            m = jnp.max(s, axis=-1, keepdims=True)
            p = jnp.exp(s - m)
            l = jnp.sum(p, axis=-1, keepdims=True)
            outs.append(_dot(p.astype(BF16), v_pair) * (1.0 / l))
        pairs.append(jnp.where(first, outs[0], outs[1]))
    o_ref[0] = jnp.concatenate(pairs, axis=-1).astype(o_ref.dtype)


def _attn_call(q, k, v):
    B, T, nq = q.shape
    TL = k.shape[1]
    R = ATTN_ROW_TILE
    return pl.pallas_call(
        _attn_kernel,
        grid=(B, T // R),
        in_specs=[
            pl.BlockSpec((1, R, nq), lambda b, i: (b, i, 0)),
            pl.BlockSpec((1, TL, nq), lambda b, i: (b, 0, 0)),
            pl.BlockSpec((1, TL, v.shape[2]), lambda b, i: (b, 0, 0)),
        ],
        out_specs=pl.BlockSpec((1, R, MLA_HEADS * MLA_V), lambda b, i: (b, i, 0)),
        out_shape=jax.ShapeDtypeStruct((B, T, MLA_HEADS * MLA_V), BF16),
        compiler_params=pltpu.CompilerParams(dimension_semantics=("arbitrary", "arbitrary"),
                                             vmem_limit_bytes=VMEM_LIMIT),
        name="mla_attn",
    )(q, k, v)


def _mlp_kernel(n_ff_chunks, x_ref, om_ref, oh_ref, mod_ref, woa_ref, wob_ref, nmlp_ref, w1_ref, w2_ref,
                fn_ref, o_ref):
    mod = mod_ref[0]
    y = _dot(om_ref[0], woa_ref[...]) + _dot(oh_ref[0], wob_ref[...])
    x1 = x_ref[0] + mod[0:1] * y
    h2 = (_rms(x1, nmlp_ref[...]) * (1.0 + mod[2:3]) + mod[1:2]).astype(BF16)
    ff = w1_ref.shape[1] // n_ff_chunks
    acc = jnp.zeros(x1.shape, F32)
    for c in range(n_ff_chunks):
        u = jnp.maximum(_dot(h2, w1_ref[:, c * ff:(c + 1) * ff]), 0.0)
        acc = acc + _dot((u * u).astype(BF16), w2_ref[c * ff:(c + 1) * ff, :])
    x2 = x1 + mod[3:4] * acc
    o_ref[0] = _rms(x2, fn_ref[...])


def _mlp_call(x, o_mla, o_hg, mod4, woa, wob, nmlp, w1, w2, fnorm):
    B, T, D = x.shape
    R = MLP_ROW_TILE
    dff = w1.shape[1]
    half = o_mla.shape[2]
    row_idx = lambda b, i: (b, i, 0)
    const2 = lambda b, i: (0, 0)
    return pl.pallas_call(
        functools.partial(_mlp_kernel, 4),
        grid=(B, T // R),
        in_specs=[
            pl.BlockSpec((1, R, D), row_idx),
            pl.BlockSpec((1, R, half), row_idx),
            pl.BlockSpec((1, R, half), row_idx),
            pl.BlockSpec((1, 4, D), lambda b, i: (b, 0, 0)),
            pl.BlockSpec((half, D), const2, pipeline_mode=pl.Buffered(1)),
            pl.BlockSpec((half, D), const2, pipeline_mode=pl.Buffered(1)),
            pl.BlockSpec((1, D), const2),
            pl.BlockSpec((D, dff), const2, pipeline_mode=pl.Buffered(1)),
            pl.BlockSpec((dff, D), const2, pipeline_mode=pl.Buffered(1)),
            pl.BlockSpec((1, D), const2),
        ],
        out_specs=pl.BlockSpec((1, R, D), row_idx),
        out_shape=jax.ShapeDtypeStruct((B, T, D), F32),
        compiler_params=pltpu.CompilerParams(dimension_semantics=("arbitrary", "arbitrary"),
                                             vmem_limit_bytes=VMEM_LIMIT),
        name="out_mlp",
    )(x, o_mla, o_hg, mod4, woa, wob, nmlp, w1, w2, fnorm)


def _rope_slot(w):
    z = jnp.zeros(w.shape[:-1] + (MLA_NOPE,), w.dtype)
    z2 = jnp.zeros(w.shape[:-1] + (HEAD_SLOT - MLA_NOPE - MLA_ROPE,), w.dtype)
    return jnp.concatenate([z, w, z2], axis=-1)


def _pack_w_in(w_in):
    o = 0
    parts = []
    for n in (MLA_Q_RANK, MLA_KV_RANK, MLA_ROPE, HG_KW, HG_KW, HG_KW, HG_KW, HG_KW):
        parts.append(w_in[:, o:o + n])
        o += n
    cq, ckv, kr, hq, hf, hb, hi, hg = parts
    return jnp.concatenate([cq, ckv, _rope_slot(kr), hq, hf, hb, hi, hg], axis=1).astype(BF16)


def _pack_w_uq(w_uq):
    k = w_uq.shape[0]
    w3 = w_uq.reshape(k, MLA_HEADS, MLA_NOPE + MLA_ROPE)
    nope = w3[..., :MLA_NOPE].reshape(k, MLA_HEADS * MLA_NOPE)
    rope = w3[..., MLA_NOPE:].reshape(k, MLA_HEADS // 2, 2 * MLA_ROPE)
    rope = jnp.concatenate([jnp.zeros((k, MLA_HEADS // 2, LANES - 2 * MLA_ROPE), w_uq.dtype), rope], axis=-1)
    return jnp.concatenate([nope, rope.reshape(k, (MLA_HEADS // 2) * LANES)], axis=1).astype(BF16)


def _pack_w_ukv(w_ukv):
    k = w_ukv.shape[0]
    w3 = w_ukv.reshape(k, MLA_HEADS, MLA_NOPE + MLA_V)
    kk = w3[..., :MLA_NOPE].reshape(k, MLA_HEADS * MLA_NOPE)
    vv = w3[..., MLA_NOPE:].reshape(k, MLA_HEADS * MLA_V)
    return jnp.concatenate([kk, vv], axis=1).astype(BF16)


def _rope_tables(T):
    rows = T // GRID_W
    row = jnp.repeat(jnp.arange(rows), GRID_W).astype(F32)
    col = jnp.tile(jnp.arange(GRID_W), rows).astype(F32)
    n = MLA_ROPE // 4
    inv = ROPE_BASE ** (-jnp.arange(n, dtype=F32) / n)
    ar, ac = row[:, None] * inv, col[:, None] * inv
    cos32 = jnp.concatenate([jnp.cos(ar), jnp.cos(ar), jnp.cos(ac), jnp.cos(ac)], axis=-1)
    sin32 = jnp.concatenate([jnp.sin(ar), jnp.sin(ar), jnp.sin(ac), jnp.sin(ac)], axis=-1)
    cos_t = jnp.concatenate([jnp.ones((T, MLA_NOPE), F32), cos32, cos32], axis=-1)
    sin_t = jnp.concatenate([jnp.zeros((T, MLA_NOPE), F32), sin32, sin32], axis=-1)
    return cos_t, sin_t


def _scan_constants(reverse):
    R = ROW_TILE
    t = jnp.arange(R, dtype=jnp.int32)[:, None]
    s = jnp.arange(R, dtype=jnp.int32)[None, :]
    if reverse:
        t, s = s, t
    tri = (t >= s).astype(BF16)
    H = R // 2
    t, s = jnp.broadcast_to(t, (R, R))[:H, :H], jnp.broadcast_to(s, (R, R))[:H, :H]
    x = t ^ s
    lev = jnp.zeros((H, H), jnp.int32)
    for bit in range(1, N_LEVELS - 1):
        lev = jnp.where((x >> bit) > 0, bit, lev)
    lv = jnp.where(t > s, lev, jnp.where(t == s, N_LEVELS - 1, -1)).astype(jnp.int32)
    return tri, lv


def kernel(x, c, ctx, c_ctx, w_ada, b_ada, norm_mix, w_in, q_norm, w_uq, kv_norm, w_ukv, hgrn_lb, hgrn_norm,
           w_out, norm_mlp, w_mlp_in, w_mlp_out, final_norm):
    B, T, D = x.shape
    L = ctx.shape[1]
    assert w_ada.shape[0] == 1, "single-layer block"
    assert T % ROW_TILE == 0 and L == ROW_TILE and T % GRID_W == 0
    assert T % MLP_ROW_TILE == 0 and T % ATTN_ROW_TILE == 0 and T % PROJ_ROW_TILE == 0
    assert PROJ_ROW_TILE % L == 0 and B % (PROJ_ROW_TILE // L) == 0 and T % L == 0
    n_lat = T // ROW_TILE

    pad_rows = (-(B + 1)) % 8
    cc = jnp.concatenate([c, c_ctx[None, :], jnp.zeros((pad_rows, D), F32)], axis=0)
    mod = _ada_call(cc, w_ada[0], b_ada[0][None, :])
    mod_lat = mod[:B].reshape(B, 6, D)
    mod_ctx = mod[B].reshape(1, 6, D)
    mod4 = mod_lat[:, 2:6]

    cos_t, sin_t = _rope_tables(T)
    win_p, wkv_p = _pack_w_in(w_in[0]), _pack_w_ukv(w_ukv[0])
    q, k, v, qh, lff, lfb, vi, g = _inproj_lat_call(
        x, mod_lat[:, 0:2], norm_mix, win_p, q_norm, _pack_w_uq(w_uq[0]), kv_norm, wkv_p, hgrn_lb, cos_t, sin_t, L)
    k, v, qh, lff, lfb, vi = _inproj_ctx_call(
        ctx, mod_ctx[:, 0:2], norm_mix, win_p, kv_norm, wkv_p, hgrn_lb, (k, v, qh, lff, lfb, vi), T)

    tri_f, lv_f = _scan_constants(False)
    tri_b, lv_b = _scan_constants(True)
    o_f = _hgrn_call(False, qh, lff, vi, tri_f, lv_f, n_lat)
    o_hg = _hgrn_call(True, qh, lfb, vi, tri_b, lv_b, n_lat, extra=(o_f, g, hgrn_norm))

    o_mla = _attn_call(q, k, v)

    half = MLA_HEADS * MLA_V
    wo = w_out[0].astype(BF16)
    return _mlp_call(x, o_mla, o_hg, mod4, wo[:half], wo[half:], norm_mlp, w_mlp_in[0].astype(BF16),
                     w_mlp_out[0].astype(BF16), final_norm[None, :])
```

```python
import functools

import jax
import jax.numpy as jnp
import numpy as np
from jax import lax
from jax.experimental import pallas as pl
from jax.experimental.pallas import tpu as pltpu

F32 = jnp.float32
BF16 = jnp.bfloat16

GRID_W = 64
MLA_HEADS = 8
MLA_Q_RANK = 256
MLA_KV_RANK = 128
MLA_NOPE = 64
MLA_ROPE = 32
MLA_V = 64
HG_HEADS = 4
HG_DK = 128
HG_DV = 128
HG_KW = HG_HEADS * HG_DK
ROPE_BASE = 10000.0
EPS = 1e-6
LOG2E = 1.4426950408889634

LANES = 128
SUBLANES = 8
HEAD_SLOT = LANES
ROW_TILE = 256
MLP_ROW_TILE = 512
PROJ_ROW_TILE = 512
ATTN_ROW_TILE = 512
VMEM_LIMIT = 56 * 1024 * 1024

_C_CQ = (0, 256)
_C_CKV = (256, 384)
_C_KR = (384, 512)
_C_HQ = (512, 1024)
_C_HF = (1024, 1536)
_C_HB = (1536, 2048)
_C_HI = (2048, 2560)
_C_HG = (2560, 3072)
IN_P_WIDTH = 3072


def _rms(x, g):
    ms = jnp.mean(x * x, axis=-1, keepdims=True)
    return (x * lax.rsqrt(ms + EPS)) * g


def _silu(x):
    return x * jax.nn.sigmoid(x)


def _dot(a, b):
    return jnp.dot(a, b, preferred_element_type=F32)


def _dot_nt(a, b):
    return lax.dot_general(a, b, (((1,), (1,)), ((), ())), preferred_element_type=F32)


def _dot_tn(a, b):
    return lax.dot_general(a, b, (((0,), (0,)), ((), ())), preferred_element_type=F32)


def _ada_kernel(c_ref, w_ref, b_ref, o_ref):
    s = _silu(c_ref[...])
    o_ref[...] = jnp.dot(s, w_ref[...], preferred_element_type=F32,
                         precision=lax.Precision.HIGHEST) + b_ref[...]


def _ada_call(cc, w_ada, b_ada):
    rows, d = cc.shape
    n = w_ada.shape[1]
    tn = 1024
    return pl.pallas_call(
        _ada_kernel,
        grid=(n // tn,),
        in_specs=[
            pl.BlockSpec((rows, d), lambda j: (0, 0)),
            pl.BlockSpec((d, tn), lambda j: (0, j)),
            pl.BlockSpec((1, tn), lambda j: (0, j)),
        ],
        out_specs=pl.BlockSpec((rows, tn), lambda j: (0, j)),
        out_shape=jax.ShapeDtypeStruct((rows, n), F32),
        compiler_params=pltpu.CompilerParams(dimension_semantics=("arbitrary",), vmem_limit_bytes=VMEM_LIMIT),
        name="ada_mod",
    )(cc, w_ada, b_ada)


def _lane_lt(n_rows, bound):
    return lax.broadcasted_iota(jnp.int32, (n_rows, LANES), 1) < bound


def _rotate_half_lanes(x):
    lane = lax.broadcasted_iota(jnp.int32, x.shape, 1)
    return jnp.where((lane & 15) < 8, -pltpu.roll(x, LANES - 8, 1), pltpu.roll(x, 8, 1))


def _head_slots(nope, rope_lo, rope_hi):
    first = _lane_lt(nope.shape[0], MLA_NOPE)
    return (jnp.where(first, nope, rope_lo), jnp.where(first, pltpu.roll(nope, MLA_NOPE, 1), rope_hi))


def _inproj_kernel(is_ctx, *refs):
    if is_ctx:
        (x_ref, mod_ref, nmix_ref, win_ref, kvn_ref, wkv_ref, lb_ref, _k, _v, _qh, _lff, _lfb, _vi,
         k_out, v_out, qh_out, lff_out, lfb_out, vi_out) = refs
    else:
        (x_ref, mod_ref, nmix_ref, win_ref, qn_ref, wq_ref, kvn_ref, wkv_ref, lb_ref, cos_ref, sin_ref,
         q_out, k_out, v_out, qh_out, lff_out, lfb_out, vi_out, g_out) = refs
    mod = mod_ref[0]
    h = _rms(x_ref[0], nmix_ref[...]) * (1.0 + mod[1:2]) + mod[0:1]
    hb = h.astype(BF16)
    n_pairs = MLA_HEADS // 2
    half = n_pairs * LANES

    def proj(c):
        return _dot(hb, win_ref[:, c[0]:c[1]])

    def put(ref, val):
        ref[...] = val.astype(ref.dtype).reshape(ref.shape)

    def pair(a, j):
        return a[:, j * LANES:(j + 1) * LANES]

    lbr = lb_ref[...]
    e = jnp.exp(lbr - jnp.max(lbr, axis=0, keepdims=True))
    lb = e[0] / jnp.sum(e, axis=0)
    lbf = lb[0:1]
    lbb = lb[1:2]

    ckv = proj(_C_CKV)
    kr = proj(_C_KR)
    if not is_ctx:
        cq = proj(_C_CQ)
    put(qh_out, _silu(proj(_C_HQ)))
    kv = _dot(_rms(ckv, kvn_ref[...]).astype(BF16), wkv_ref[...])
    put(lff_out, jnp.log(lbf + (1.0 - lbf) * jax.nn.sigmoid(proj(_C_HF))))
    if not is_ctx:
        cos = cos_ref[...]
        sin = sin_ref[...]
        kr = kr * cos + _rotate_half_lanes(kr) * sin
        qq = _dot(_rms(cq, qn_ref[...]).astype(BF16), wq_ref[...])
    put(lfb_out, jnp.log(lbb + (1.0 - lbb) * jax.nn.sigmoid(proj(_C_HB))))
    k_slots = []
    for j in range(n_pairs):
        k_slots += _head_slots(pair(kv, j), kr, kr)
    put(k_out, jnp.concatenate(k_slots, axis=1))
    put(v_out, kv[:, half:])
    put(vi_out, proj(_C_HI))
    if not is_ctx:
        scale = (MLA_NOPE + MLA_ROPE) ** -0.5 * LOG2E
        q_slots = []
        for j in range(n_pairs):
            r = pair(qq, n_pairs + j)
            r = (r * cos + _rotate_half_lanes(r) * sin) * scale
            q_slots += _head_slots(pair(qq, j) * scale, r, pltpu.roll(r, LANES - MLA_ROPE, 1))
        put(q_out, jnp.concatenate(q_slots, axis=1))
        put(g_out, _silu(proj(_C_HG)))


def _inproj_lat_call(x, mod_lat, nmix, win_p, qn, wq_p, kvn, wkv_p, lb_raw, cos_t, sin_t, L):
    B, T, D = x.shape
    R = PROJ_ROW_TILE
    TL = T + L
    nq = MLA_HEADS * HEAD_SLOT
    nv = MLA_HEADS * MLA_V
    row_idx = lambda b, i: (b, i, 0)
    const2 = lambda b, i: (0, 0)
    in_specs = [
        pl.BlockSpec((1, R, D), row_idx),
        pl.BlockSpec((1, 2, D), lambda b, i: (b, 0, 0)),
        pl.BlockSpec((1, D), const2),
        pl.BlockSpec((D, IN_P_WIDTH), const2, pipeline_mode=pl.Buffered(1)),
        pl.BlockSpec((1, MLA_Q_RANK), const2),
        pl.BlockSpec((MLA_Q_RANK, nq), const2, pipeline_mode=pl.Buffered(1)),
        pl.BlockSpec((1, MLA_KV_RANK), const2),
        pl.BlockSpec((MLA_KV_RANK, nq), const2, pipeline_mode=pl.Buffered(1)),
        pl.BlockSpec(lb_raw.shape, lambda b, i: (0, 0, 0)),
        pl.BlockSpec((R, LANES), lambda b, i: (i, 0)),
        pl.BlockSpec((R, LANES), lambda b, i: (i, 0)),
    ]
    out_shape = [
        jax.ShapeDtypeStruct((B, T, nq), BF16),
        jax.ShapeDtypeStruct((B, TL, nq), BF16),
        jax.ShapeDtypeStruct((B, TL, nv), BF16),
        jax.ShapeDtypeStruct((B, TL, HG_KW), BF16),
        jax.ShapeDtypeStruct((B, TL, HG_KW), F32),
        jax.ShapeDtypeStruct((B, TL, HG_KW), F32),
        jax.ShapeDtypeStruct((B, TL, HG_KW), BF16),
        jax.ShapeDtypeStruct((B, T, HG_KW), BF16),
    ]
    out_specs = ([pl.BlockSpec((1, R, nq), row_idx)] * 2 + [pl.BlockSpec((1, R, nv), row_idx)]
                 + [pl.BlockSpec((1, R, HG_KW), row_idx)] * 5)
    return pl.pallas_call(
        functools.partial(_inproj_kernel, False),
        grid=(B, T // R),
        in_specs=in_specs,
        out_specs=out_specs,
        out_shape=out_shape,
        compiler_params=pltpu.CompilerParams(dimension_semantics=("arbitrary", "arbitrary"),
                                             vmem_limit_bytes=VMEM_LIMIT),
        name="inproj_lat",
    )(x, mod_lat, nmix, win_p, qn, wq_p, kvn, wkv_p, lb_raw, cos_t, sin_t)


def _inproj_ctx_call(ctx, mod_ctx, nmix, win_p, kvn, wkv_p, lb_raw, dests, T):
    B, L, D = ctx.shape
    R = PROJ_ROW_TILE
    per = R // L
    nq = MLA_HEADS * HEAD_SLOT
    x2 = ctx.reshape(B // per, R, D)
    const2 = lambda i: (0, 0)
    in_specs = [
        pl.BlockSpec((1, R, D), lambda i: (i, 0, 0)),
        pl.BlockSpec((1, 2, D), lambda i: (0, 0, 0)),
        pl.BlockSpec((1, D), const2),
        pl.BlockSpec((D, IN_P_WIDTH), const2, pipeline_mode=pl.Buffered(1)),
        pl.BlockSpec((1, MLA_KV_RANK), const2),
        pl.BlockSpec((MLA_KV_RANK, nq), const2, pipeline_mode=pl.Buffered(1)),
        pl.BlockSpec(lb_raw.shape, lambda i: (0, 0, 0)),
    ] + [pl.BlockSpec(memory_space=pl.ANY)] * len(dests)
    n_fixed = len(in_specs) - len(dests)
    ctx_blk = lambda i: (i, T // L, 0)
    out_specs = [pl.BlockSpec((per, L, d.shape[2]), ctx_blk) for d in dests]
    return pl.pallas_call(
        functools.partial(_inproj_kernel, True),
        grid=(B // per,),
        in_specs=in_specs,
        out_specs=out_specs,
        out_shape=[jax.ShapeDtypeStruct(d.shape, d.dtype) for d in dests],
        input_output_aliases={n_fixed + j: j for j in range(len(dests))},
        compiler_params=pltpu.CompilerParams(dimension_semantics=("arbitrary",), vmem_limit_bytes=VMEM_LIMIT),
        name="inproj_ctx",
    )(x2, mod_ctx, nmix, win_p, kvn, wkv_p, lb_raw, *dests)


N_LEVELS = 8
N_SMALL_LEVELS = 2


def _neg_abs(x):
    bits = lax.bitcast_convert_type(x, jnp.uint32) | jnp.uint32(0x80000000)
    return lax.bitcast_convert_type(bits, F32)


def _replace_bit(y, row, bit, value):
    step = 1 << bit
    n = y.shape[0]
    has = ((row >> bit) & 1) == 1
    if value == 1:
        return jnp.where(has, y, pltpu.roll(y, n - step, 0))
    return jnp.where(has, pltpu.roll(y, step, 0), y)


def _hgrn_kernel(reverse, qh_ref, lf_ref, vi_ref, tri_ref, lv_ref, *rest):
    if reverse:
        of_ref, g_ref, hgn_ref, out_ref, st_ref = rest
    else:
        out_ref, st_ref = rest
    i = pl.program_id(1)

    @pl.when(i == 0)
    def _():
        st_ref[...] = jnp.zeros_like(st_ref)

    lf = lf_ref[0]
    l_hi = lf.astype(BF16)
    r1 = lf - l_hi.astype(F32)
    l_mid = r1.astype(BF16)
    l_lo = (r1 - l_mid.astype(F32)).astype(BF16)
    tri = tri_ref[...]
    b_all = (_dot(tri, l_hi) + _dot(tri, l_mid) + _dot(tri, l_lo)) * LOG2E
    k_all = 1.0 - jnp.exp(lf)
    q_all = qh_ref[0].astype(F32)
    lv = lv_ref[...]
    R = lf.shape[0]
    H = R // 2
    row = lax.broadcasted_iota(jnp.int32, (R, LANES), 0)
    near, far = (0, 1) if not reverse else (1, 0)
    end_row = 0 if reverse else R - 1
    heads = range(HG_HEADS)
    sls = [slice(h * HG_DK, (h + 1) * HG_DK) for h in heads]
    bs = [b_all[:, sl] for sl in sls]
    qs = [q_all[:, sl] for sl in sls]
    ks = [k_all[:, sl] for sl in sls]
    vs = [vi_ref[0, :, sl] for sl in sls]

    diag = lv == N_LEVELS - 1
    a0 = [jnp.where(diag, _dot_nt(qs[h][:H].astype(BF16), ks[h][:H].astype(BF16)), 0.0) for h in heads]
    a1 = [jnp.where(diag, _dot_nt(qs[h][H:].astype(BF16), ks[h][H:].astype(BF16)), 0.0) for h in heads]
    a_off = [None] * HG_HEADS
    zs = list(bs)
    for lev in range(N_LEVELS):
        m = 1 << lev
        mask = lv == lev
        q_side = ((row >> lev) & 1) == far
        for h in heads:
            bh, qh, kh = bs[h], qs[h], ks[h]
            if lev < N_SMALL_LEVELS:
                bmid = _replace_bit(zs[h], row, lev, near)
                zs[h] = _replace_bit(zs[h], row, lev, far)
                qk = jnp.where(q_side, qh, kh)
                arg = _neg_abs(bh - bmid)
            elif 2 * m == SUBLANES:
                ref = m if reverse else m - 1
                bmid = jnp.concatenate([jnp.broadcast_to(bh[lo + ref:lo + ref + 1], (SUBLANES, HG_DK))
                                        for lo in range(0, R, SUBLANES)], axis=0)
                qk = jnp.where(q_side, qh, kh)
                arg = _neg_abs(bh - bmid)
            else:
                args, parts = [], []
                for j in range(R // (2 * m)):
                    lo = j * 2 * m
                    ref = lo + (m if reverse else m - 1)
                    bref = bh[ref:ref + 1]
                    if reverse:
                        args += [bh[lo:lo + m] - bref, bref - bh[lo + m:lo + 2 * m]]
                        parts += [qh[lo:lo + m], kh[lo + m:lo + 2 * m]]
                    else:
                        args += [bref - bh[lo:lo + m], bh[lo + m:lo + 2 * m] - bref]
                        parts += [kh[lo:lo + m], qh[lo + m:lo + 2 * m]]
                arg = jnp.concatenate(args, axis=0)
                qk = jnp.concatenate(parts, axis=0)
            u = (qk * jnp.exp2(arg)).astype(BF16)
            if lev < N_LEVELS - 1:
                a0[h] = jnp.where(mask, _dot_nt(u[:H], u[:H]), a0[h])
                a1[h] = jnp.where(mask, _dot_nt(u[H:], u[H:]), a1[h])
            elif reverse:
                a_off[h] = _dot_nt(u[:H], u[H:])
            else:
                a_off[h] = _dot_nt(u[H:], u[:H])

    for h in heads:
        sl, bh, qh, kh, vh = sls[h], bs[h], qs[h], ks[h], vs[h]
        zend = bh[end_row:end_row + 1]
        st = st_ref[h]
        if reverse:
            o_in = jnp.concatenate([
                _dot(jnp.concatenate([a0[h], a_off[h]], axis=1).astype(BF16), vh),
                _dot(a1[h].astype(BF16), vh[H:])], axis=0)
        else:
            o_in = jnp.concatenate([
                _dot(a0[h].astype(BF16), vh[:H]),
                _dot(jnp.concatenate([a_off[h], a1[h]], axis=1).astype(BF16), vh)], axis=0)
        o = o_in + _dot_nt((qh * jnp.exp2(bh)).astype(BF16), st.astype(BF16))
        k_end = (kh * jnp.exp2(zend - bh)).astype(BF16)
        st_ref[h] = jnp.exp2(zend) * st + _dot_tn(vh, k_end)

        if reverse:
            tot = of_ref[0, :, sl] + o
            out_ref[0, :, sl] = (_rms(tot, hgn_ref[...]) * g_ref[0, :, sl].astype(F32)).astype(out_ref.dtype)
        else:
            out_ref[0, :, sl] = o


def _hgrn_call(reverse, qh, lf, vi, tri, lv, n_lat, extra=()):
    B, TL, KW = qh.shape
    R = ROW_TILE
    n_all = TL // R
    T = n_lat * R

    if reverse:
        def seq_idx(b, i):
            return (b, jnp.where(i == 0, n_all - 1, n_lat - i), 0)

        def lat_idx(b, i):
            return (b, jnp.where(i == 0, n_lat - 1, n_lat - i), 0)
    else:
        def seq_idx(b, i):
            return (b, jnp.where(i == 0, n_all - 1, i - 1), 0)

        def lat_idx(b, i):
            return (b, jnp.maximum(i - 1, 0), 0)

    const2 = lambda b, i: (0, 0)
    in_specs = [
        pl.BlockSpec((1, R, KW), seq_idx),
        pl.BlockSpec((1, R, KW), seq_idx),
        pl.BlockSpec((1, R, KW), seq_idx),
        pl.BlockSpec((R, R), const2),
        pl.BlockSpec((R // 2, R // 2), const2),
    ]
    if reverse:
        in_specs += [
            pl.BlockSpec((1, R, KW), lat_idx),
            pl.BlockSpec((1, R, KW), lat_idx),
            pl.BlockSpec((1, HG_DV), const2),
        ]
        out_dtype = BF16
    else:
        out_dtype = F32
    return pl.pallas_call(
        functools.partial(_hgrn_kernel, reverse),
        grid=(B, n_lat + 1),
        in_specs=in_specs,
        out_specs=pl.BlockSpec((1, R, KW), lat_idx),
        out_shape=jax.ShapeDtypeStruct((B, T, KW), out_dtype),
        scratch_shapes=[pltpu.VMEM((HG_HEADS, HG_DV, HG_DK), F32)],
        compiler_params=pltpu.CompilerParams(dimension_semantics=("arbitrary", "arbitrary"),
                                             vmem_limit_bytes=VMEM_LIMIT),
        name="hgrn_bwd" if reverse else "hgrn_fwd",
    )(qh, lf, vi, tri, lv, *extra)


def _attn_kernel(q_ref, k_ref, v_ref, o_ref):
    def scores(h):
        sl = slice(h * HEAD_SLOT, (h + 1) * HEAD_SLOT)
        return _dot_nt(q_ref[0, :, sl], k_ref[0, :, sl])

    def attend(h, s):
        v_pair = v_ref[0, :, (h // 2) * LANES:(h // 2 + 1) * LANES]
        p = jnp.exp2(s - jnp.max(s, axis=-1, keepdims=True))
        l = jnp.sum(p, axis=-1, keepdims=True)
        return _dot(p.astype(BF16), v_pair) * (1.0 / l)

    outs = []
    s_next = scores(0)
    for h in range(MLA_HEADS):
        s = s_next
        if h + 1 < MLA_HEADS:
            s_next = scores(h + 1)
        outs.append(attend(h, s))
    first = _lane_lt(q_ref.shape[1], MLA_V)
    pairs = [jnp.where(first, outs[2 * j], outs[2 * j + 1]) for j in range(MLA_HEADS // 2)]
    o_ref[0] = jnp.concatenate(pairs, axis=-1).astype(o_ref.dtype)


def _attn_call(q, k, v):
    B, T, nq = q.shape
    TL = k.shape[1]
    R = ATTN_ROW_TILE
    return pl.pallas_call(
        _attn_kernel,
        grid=(B, T // R),
        in_specs=[
            pl.BlockSpec((1, R, nq), lambda b, i: (b, i, 0)),
            pl.BlockSpec((1, TL, nq), lambda b, i: (b, 0, 0)),
            pl.BlockSpec((1, TL, v.shape[2]), lambda b, i: (b, 0, 0)),
        ],
        out_specs=pl.BlockSpec((1, R, MLA_HEADS * MLA_V), lambda b, i: (b, i, 0)),
        out_shape=jax.ShapeDtypeStruct((B, T, MLA_HEADS * MLA_V), BF16),
        compiler_params=pltpu.CompilerParams(dimension_semantics=("arbitrary", "arbitrary"),
                                             vmem_limit_bytes=VMEM_LIMIT),
        name="mla_attn",
    )(q, k, v)


def _mlp_kernel(n_ff_chunks, x_ref, om_ref, oh_ref, mod_ref, woa_ref, wob_ref, nmlp_ref, w1_ref, w2_ref,
                fn_ref, o_ref):
    mod = mod_ref[0]
    y = _dot(om_ref[0], woa_ref[...]) + _dot(oh_ref[0], wob_ref[...])
    x1 = x_ref[0] + mod[0:1] * y
    h2 = (_rms(x1, nmlp_ref[...]) * (1.0 + mod[2:3]) + mod[1:2]).astype(BF16)
    ff = w1_ref.shape[1] // n_ff_chunks
    acc = jnp.zeros(x1.shape, F32)
    for c in range(n_ff_chunks):
        u = jnp.maximum(_dot(h2, w1_ref[:, c * ff:(c + 1) * ff]), 0.0)
        acc = acc + _dot((u * u).astype(BF16), w2_ref[c * ff:(c + 1) * ff, :])
    x2 = x1 + mod[3:4] * acc
    o_ref[0] = _rms(x2, fn_ref[...])


def _mlp_call(x, o_mla, o_hg, mod4, woa, wob, nmlp, w1, w2, fnorm):
    B, T, D = x.shape
    R = MLP_ROW_TILE
    dff = w1.shape[1]
    half = o_mla.shape[2]
    row_idx = lambda b, i: (b, i, 0)
    const2 = lambda b, i: (0, 0)
    return pl.pallas_call(
        functools.partial(_mlp_kernel, 4),
        grid=(B, T // R),
        in_specs=[
            pl.BlockSpec((1, R, D), row_idx),
            pl.BlockSpec((1, R, half), row_idx),
            pl.BlockSpec((1, R, half), row_idx),
            pl.BlockSpec((1, 4, D), lambda b, i: (b, 0, 0)),
            pl.BlockSpec((half, D), const2, pipeline_mode=pl.Buffered(1)),
            pl.BlockSpec((half, D), const2, pipeline_mode=pl.Buffered(1)),
            pl.BlockSpec((1, D), const2),
            pl.BlockSpec((D, dff), const2, pipeline_mode=pl.Buffered(1)),
            pl.BlockSpec((dff, D), const2, pipeline_mode=pl.Buffered(1)),
            pl.BlockSpec((1, D), const2),
        ],
        out_specs=pl.BlockSpec((1, R, D), row_idx),
        out_shape=jax.ShapeDtypeStruct((B, T, D), F32),
        compiler_params=pltpu.CompilerParams(dimension_semantics=("arbitrary", "arbitrary"),
                                             vmem_limit_bytes=VMEM_LIMIT),
        name="out_mlp",
    )(x, o_mla, o_hg, mod4, woa, wob, nmlp, w1, w2, fnorm)


def _rope_slot(w):
    z = jnp.zeros(w.shape[:-1] + (MLA_NOPE,), w.dtype)
    z2 = jnp.zeros(w.shape[:-1] + (HEAD_SLOT - MLA_NOPE - MLA_ROPE,), w.dtype)
    return jnp.concatenate([z, w, z2], axis=-1)


def _pack_w_in(w_in):
    o = 0
    parts = []
    for n in (MLA_Q_RANK, MLA_KV_RANK, MLA_ROPE, HG_KW, HG_KW, HG_KW, HG_KW, HG_KW):
        parts.append(w_in[:, o:o + n])
        o += n
    cq, ckv, kr, hq, hf, hb, hi, hg = parts
    return jnp.concatenate([cq, ckv, _rope_slot(kr), hq, hf, hb, hi, hg], axis=1).astype(BF16)


def _pack_w_uq(w_uq):
    k = w_uq.shape[0]
    w3 = w_uq.reshape(k, MLA_HEADS, MLA_NOPE + MLA_ROPE)
    nope = w3[..., :MLA_NOPE].reshape(k, MLA_HEADS * MLA_NOPE)
    rope = w3[..., MLA_NOPE:].reshape(k, MLA_HEADS // 2, 2 * MLA_ROPE)
    rope = jnp.concatenate([jnp.zeros((k, MLA_HEADS // 2, LANES - 2 * MLA_ROPE), w_uq.dtype), rope], axis=-1)
    return jnp.concatenate([nope, rope.reshape(k, (MLA_HEADS // 2) * LANES)], axis=1).astype(BF16)


def _pack_w_ukv(w_ukv):
    k = w_ukv.shape[0]
    w3 = w_ukv.reshape(k, MLA_HEADS, MLA_NOPE + MLA_V)
    kk = w3[..., :MLA_NOPE].reshape(k, MLA_HEADS * MLA_NOPE)
    vv = w3[..., MLA_NOPE:].reshape(k, MLA_HEADS * MLA_V)
    return jnp.concatenate([kk, vv], axis=1).astype(BF16)


def _rope_tables(T):
    rows = T // GRID_W
    row = np.repeat(np.arange(rows), GRID_W).astype(np.float64)
    col = np.tile(np.arange(GRID_W), rows).astype(np.float64)
    n = MLA_ROPE // 4
    inv = ROPE_BASE ** (-np.arange(n, dtype=np.float64) / n)
    ar, ac = row[:, None] * inv, col[:, None] * inv
    cos32 = np.concatenate([np.cos(ar), np.cos(ar), np.cos(ac), np.cos(ac)], axis=-1)
    sin32 = np.concatenate([np.sin(ar), np.sin(ar), np.sin(ac), np.sin(ac)], axis=-1)
    cos_t = np.concatenate([np.ones((T, MLA_NOPE)), cos32, cos32], axis=-1)
    sin_t = np.concatenate([np.zeros((T, MLA_NOPE)), sin32, sin32], axis=-1)
    return jnp.asarray(cos_t, F32), jnp.asarray(sin_t, F32)


def _scan_constants(reverse):
    R = ROW_TILE
    t = np.arange(R, dtype=np.int32)[:, None]
    s = np.arange(R, dtype=np.int32)[None, :]
    if reverse:
        t, s = s, t
    tri = (t >= s)
    H = R // 2
    t, s = np.broadcast_to(t, (R, R))[:H, :H], np.broadcast_to(s, (R, R))[:H, :H]
    x = t ^ s
    lev = np.zeros((H, H), np.int32)
    for bit in range(1, N_LEVELS - 1):
        lev = np.where((x >> bit) > 0, bit, lev)
    lv = np.where(t > s, lev, np.where(t == s, N_LEVELS - 1, -1))
    return jnp.asarray(tri, BF16), jnp.asarray(lv, jnp.int32)


def kernel(x, c, ctx, c_ctx, w_ada, b_ada, norm_mix, w_in, q_norm, w_uq, kv_norm, w_ukv, hgrn_lb, hgrn_norm,
           w_out, norm_mlp, w_mlp_in, w_mlp_out, final_norm):
    B, T, D = x.shape
    L = ctx.shape[1]
    assert w_ada.shape[0] == 1, "single-layer block"
    assert T % ROW_TILE == 0 and L == ROW_TILE and T % GRID_W == 0
    assert T % MLP_ROW_TILE == 0 and T % ATTN_ROW_TILE == 0 and T % PROJ_ROW_TILE == 0
    assert PROJ_ROW_TILE % L == 0 and B % (PROJ_ROW_TILE // L) == 0 and T % L == 0
    n_lat = T // ROW_TILE

    pad_rows = (-(B + 1)) % 8
    cc = jnp.concatenate([c, c_ctx[None, :], jnp.zeros((pad_rows, D), F32)], axis=0)
    mod = _ada_call(cc, w_ada[0], b_ada[0][None, :])
    mod_lat = mod[:B].reshape(B, 6, D)
    mod_ctx = mod[B].reshape(1, 6, D)
    mod4 = mod_lat[:, 2:6]

    cos_t, sin_t = _rope_tables(T)
    win_p, wkv_p = _pack_w_in(w_in[0]), _pack_w_ukv(w_ukv[0])
    q, k, v, qh, lff, lfb, vi, g = _inproj_lat_call(
        x, mod_lat[:, 0:2], norm_mix, win_p, q_norm, _pack_w_uq(w_uq[0]), kv_norm, wkv_p, hgrn_lb, cos_t, sin_t, L)
    k, v, qh, lff, lfb, vi = _inproj_ctx_call(
        ctx, mod_ctx[:, 0:2], norm_mix, win_p, kv_norm, wkv_p, hgrn_lb, (k, v, qh, lff, lfb, vi), T)

    tri_f, lv_f = _scan_constants(False)
    tri_b, lv_b = _scan_constants(True)
    o_f = _hgrn_call(False, qh, lff, vi, tri_f, lv_f, n_lat)
    o_hg = _hgrn_call(True, qh, lfb, vi, tri_b, lv_b, n_lat, extra=(o_f, g, hgrn_norm))

    o_mla = _attn_call(q, k, v)

    half = MLA_HEADS * MLA_V
    wo = w_out[0].astype(BF16)
    return _mlp_call(x, o_mla, o_hg, mod4, wo[:half], wo[half:], norm_mlp, w_mlp_in[0].astype(BF16),
                     w_mlp_out[0].astype(BF16), final_norm[None, :])
```

```python
import functools

import jax
import jax.numpy as jnp
import numpy as np
from jax import lax
from jax.experimental import pallas as pl
from jax.experimental.pallas import tpu as pltpu

F32 = jnp.float32
BF16 = jnp.bfloat16

GRID_W = 64
MLA_HEADS = 8
MLA_Q_RANK = 256
MLA_KV_RANK = 128
MLA_NOPE = 64
MLA_ROPE = 32
MLA_V = 64
HG_HEADS = 4
HG_DK = 128
HG_DV = 128
HG_KW = HG_HEADS * HG_DK
ROPE_BASE = 10000.0
EPS = 1e-6
LOG2E = 1.4426950408889634

LANES = 128
SUBLANES = 8
HEAD_SLOT = LANES
ROW_TILE = 256
MLP_ROW_TILE = 512
PROJ_ROW_TILE = 512
ATTN_ROW_TILE = 512
VMEM_LIMIT = 56 * 1024 * 1024

_C_CQ = (0, 256)
_C_CKV = (256, 384)
_C_KR = (384, 512)
_C_HQ = (512, 1024)
_C_HF = (1024, 1536)
_C_HB = (1536, 2048)
_C_HI = (2048, 2560)
_C_HG = (2560, 3072)
IN_P_WIDTH = 3072


def _rms(x, g):
    ms = jnp.mean(x * x, axis=-1, keepdims=True)
    return (x * lax.rsqrt(ms + EPS)) * g


def _silu(x):
    return x * jax.nn.sigmoid(x)


def _dot(a, b):
    return jnp.dot(a, b, preferred_element_type=F32)


def _dot_nt(a, b):
    return lax.dot_general(a, b, (((1,), (1,)), ((), ())), preferred_element_type=F32)


def _dot_tn(a, b):
    return lax.dot_general(a, b, (((0,), (0,)), ((), ())), preferred_element_type=F32)


def _ada_kernel(c_ref, w_ref, b_ref, o_ref):
    s = _silu(c_ref[...])
    o_ref[...] = jnp.dot(s, w_ref[...], preferred_element_type=F32,
                         precision=lax.Precision.HIGHEST) + b_ref[...]


def _ada_call(cc, w_ada, b_ada):
    rows, d = cc.shape
    n = w_ada.shape[1]
    tn = 1024
    return pl.pallas_call(
        _ada_kernel,
        grid=(n // tn,),
        in_specs=[
            pl.BlockSpec((rows, d), lambda j: (0, 0)),
            pl.BlockSpec((d, tn), lambda j: (0, j)),
            pl.BlockSpec((1, tn), lambda j: (0, j)),
        ],
        out_specs=pl.BlockSpec((rows, tn), lambda j: (0, j)),
        out_shape=jax.ShapeDtypeStruct((rows, n), F32),
        compiler_params=pltpu.CompilerParams(dimension_semantics=("arbitrary",), vmem_limit_bytes=VMEM_LIMIT),
        name="ada_mod",
    )(cc, w_ada, b_ada)


def _lane_lt(n_rows, bound):
    return lax.broadcasted_iota(jnp.int32, (n_rows, LANES), 1) < bound


def _rotate_half_lanes(x):
    lane = lax.broadcasted_iota(jnp.int32, x.shape, 1)
    return jnp.where((lane & 15) < 8, -pltpu.roll(x, LANES - 8, 1), pltpu.roll(x, 8, 1))


def _head_slots(nope, rope_lo, rope_hi):
    first = _lane_lt(nope.shape[0], MLA_NOPE)
    return (jnp.where(first, nope, rope_lo), jnp.where(first, pltpu.roll(nope, MLA_NOPE, 1), rope_hi))


def _inproj_kernel(is_ctx, *refs):
    if is_ctx:
        (x_ref, mod_ref, nmix_ref, win_ref, kvn_ref, wkv_ref, lb_ref,
         k_out, v_out, qh_out, lff_out, lfb_out, vi_out) = refs
    else:
        (x_ref, mod_ref, nmix_ref, win_ref, qn_ref, wq_ref, kvn_ref, wkv_ref, lb_ref, cos_ref, sin_ref,
         q_out, k_out, v_out, qh_out, lff_out, lfb_out, vi_out, g_out) = refs
    mod = mod_ref[0]
    h = _rms(x_ref[0], nmix_ref[...]) * (1.0 + mod[1:2]) + mod[0:1]
    hb = h.astype(BF16)
    n_pairs = MLA_HEADS // 2
    half = n_pairs * LANES

    def proj(c):
        return _dot(hb, win_ref[:, c[0]:c[1]])

    def put(ref, val):
        ref[...] = val.astype(ref.dtype).reshape(ref.shape)

    def pair(a, j):
        return a[:, j * LANES:(j + 1) * LANES]

    lbr = lb_ref[...]
    e = jnp.exp(lbr - jnp.max(lbr, axis=0, keepdims=True))
    lb = e[0] / jnp.sum(e, axis=0)
    lbf = lb[0:1]
    lbb = lb[1:2]

    ckv = proj(_C_CKV)
    kr = proj(_C_KR)
    if not is_ctx:
        cq = proj(_C_CQ)
    put(qh_out, _silu(proj(_C_HQ)))
    kv = _dot(_rms(ckv, kvn_ref[...]).astype(BF16), wkv_ref[...])
    put(lff_out, jnp.log(lbf + (1.0 - lbf) * jax.nn.sigmoid(proj(_C_HF))))
    if not is_ctx:
        cos = cos_ref[...]
        sin = sin_ref[...]
        kr = kr * cos + _rotate_half_lanes(kr) * sin
        qq = _dot(_rms(cq, qn_ref[...]).astype(BF16), wq_ref[...])
    put(lfb_out, jnp.log(lbb + (1.0 - lbb) * jax.nn.sigmoid(proj(_C_HB))))
    k_slots = []
    for j in range(n_pairs):
        k_slots += _head_slots(pair(kv, j), kr, kr)
    put(k_out, jnp.concatenate(k_slots, axis=1))
    put(v_out, kv[:, half:])
    put(vi_out, proj(_C_HI))
    if not is_ctx:
        scale = (MLA_NOPE + MLA_ROPE) ** -0.5 * LOG2E
        q_slots = []
        for j in range(n_pairs):
            r = pair(qq, n_pairs + j)
            r = (r * cos + _rotate_half_lanes(r) * sin) * scale
            q_slots += _head_slots(pair(qq, j) * scale, r, pltpu.roll(r, LANES - MLA_ROPE, 1))
        put(q_out, jnp.concatenate(q_slots, axis=1))
        put(g_out, _silu(proj(_C_HG)))


def _inproj_out_shapes(B, rows, latent):
    nq = MLA_HEADS * HEAD_SLOT
    nv = MLA_HEADS * MLA_V
    shapes = [
        ((B, rows, nq), BF16),
        ((B, rows, nv), BF16),
        ((B, rows, HG_KW), BF16),
        ((B, rows, HG_KW), F32),
        ((B, rows, HG_KW), F32),
        ((B, rows, HG_KW), BF16),
    ]
    if latent:
        shapes = [((B, rows, nq), BF16)] + shapes + [((B, rows, HG_KW), BF16)]
    return shapes


def _inproj_lat_call(x, mod_lat, nmix, win_p, qn, wq_p, kvn, wkv_p, lb_raw, cos_t, sin_t):
    B, T, D = x.shape
    R = PROJ_ROW_TILE
    nq = MLA_HEADS * HEAD_SLOT
    row_idx = lambda b, i: (b, i, 0)
    const2 = lambda b, i: (0, 0)
    in_specs = [
        pl.BlockSpec((1, R, D), row_idx),
        pl.BlockSpec((1, 2, D), lambda b, i: (b, 0, 0)),
        pl.BlockSpec((1, D), const2),
        pl.BlockSpec((D, IN_P_WIDTH), const2, pipeline_mode=pl.Buffered(1)),
        pl.BlockSpec((1, MLA_Q_RANK), const2),
        pl.BlockSpec((MLA_Q_RANK, nq), const2, pipeline_mode=pl.Buffered(1)),
        pl.BlockSpec((1, MLA_KV_RANK), const2),
        pl.BlockSpec((MLA_KV_RANK, nq), const2, pipeline_mode=pl.Buffered(1)),
        pl.BlockSpec(lb_raw.shape, lambda b, i: (0, 0, 0)),
        pl.BlockSpec((R, LANES), lambda b, i: (i, 0)),
        pl.BlockSpec((R, LANES), lambda b, i: (i, 0)),
    ]
    shapes = _inproj_out_shapes(B, T, True)
    return pl.pallas_call(
        functools.partial(_inproj_kernel, False),
        grid=(B, T // R),
        in_specs=in_specs,
        out_specs=[pl.BlockSpec((1, R, s[2]), row_idx) for s, _ in shapes],
        out_shape=[jax.ShapeDtypeStruct(s, d) for s, d in shapes],
        compiler_params=pltpu.CompilerParams(dimension_semantics=("arbitrary", "arbitrary"),
                                             vmem_limit_bytes=VMEM_LIMIT),
        name="inproj_lat",
    )(x, mod_lat, nmix, win_p, qn, wq_p, kvn, wkv_p, lb_raw, cos_t, sin_t)


def _inproj_ctx_call(ctx, mod_ctx, nmix, win_p, kvn, wkv_p, lb_raw):
    B, L, D = ctx.shape
    R = PROJ_ROW_TILE
    per = R // L
    nq = MLA_HEADS * HEAD_SLOT
    x2 = ctx.reshape(B // per, R, D)
    const2 = lambda i: (0, 0)
    in_specs = [
        pl.BlockSpec((1, R, D), lambda i: (i, 0, 0)),
        pl.BlockSpec((1, 2, D), lambda i: (0, 0, 0)),
        pl.BlockSpec((1, D), const2),
        pl.BlockSpec((D, IN_P_WIDTH), const2, pipeline_mode=pl.Buffered(1)),
        pl.BlockSpec((1, MLA_KV_RANK), const2),
        pl.BlockSpec((MLA_KV_RANK, nq), const2, pipeline_mode=pl.Buffered(1)),
        pl.BlockSpec(lb_raw.shape, lambda i: (0, 0, 0)),
    ]
    shapes = _inproj_out_shapes(B, L, False)
    return pl.pallas_call(
        functools.partial(_inproj_kernel, True),
        grid=(B // per,),
        in_specs=in_specs,
        out_specs=[pl.BlockSpec((per, L, s[2]), lambda i: (i, 0, 0)) for s, _ in shapes],
        out_shape=[jax.ShapeDtypeStruct(s, d) for s, d in shapes],
        compiler_params=pltpu.CompilerParams(dimension_semantics=("arbitrary",), vmem_limit_bytes=VMEM_LIMIT),
        name="inproj_ctx",
    )(x2, mod_ctx, nmix, win_p, kvn, wkv_p, lb_raw)


N_LEVELS = 8
N_SMALL_LEVELS = 2


def _neg_abs(x):
    bits = lax.bitcast_convert_type(x, jnp.uint32) | jnp.uint32(0x80000000)
    return lax.bitcast_convert_type(bits, F32)


def _replace_bit(y, row, bit, value):
    step = 1 << bit
    n = y.shape[0]
    has = ((row >> bit) & 1) == 1
    if value == 1:
        return jnp.where(has, y, pltpu.roll(y, n - step, 0))
    return jnp.where(has, pltpu.roll(y, step, 0), y)


def _hgrn_kernel(reverse, qh_ref, lf_ref, vi_ref, qhc_ref, lfc_ref, vic_ref, tri_ref, lv_ref, *rest):
    if reverse:
        of_ref, g_ref, hgn_ref, out_ref, st_ref = rest
    else:
        out_ref, st_ref = rest
    i = pl.program_id(1)

    @pl.when(i == 0)
    def _():
        st_ref[...] = jnp.zeros_like(st_ref)

    is_ctx = i == 0
    lf = jnp.where(is_ctx, lfc_ref[0], lf_ref[0])
    qh_in = jnp.where(is_ctx, qhc_ref[0], qh_ref[0])
    vi_in = jnp.where(is_ctx, vic_ref[0], vi_ref[0])
    l_hi = lf.astype(BF16)
    r1 = lf - l_hi.astype(F32)
    l_mid = r1.astype(BF16)
    l_lo = (r1 - l_mid.astype(F32)).astype(BF16)
    tri = tri_ref[...]
    b_all = (_dot(tri, l_hi) + _dot(tri, l_mid) + _dot(tri, l_lo)) * LOG2E
    k_all = 1.0 - jnp.exp(lf)
    q_all = qh_in.astype(F32)
    lv = lv_ref[...]
    R = lf.shape[0]
    H = R // 2
    row = lax.broadcasted_iota(jnp.int32, (R, LANES), 0)
    near, far = (0, 1) if not reverse else (1, 0)
    end_row = 0 if reverse else R - 1
    heads = range(HG_HEADS)
    sls = [slice(h * HG_DK, (h + 1) * HG_DK) for h in heads]
    bs = [b_all[:, sl] for sl in sls]
    qs = [q_all[:, sl] for sl in sls]
    ks = [k_all[:, sl] for sl in sls]
    vs = [vi_in[:, sl] for sl in sls]

    diag = lv == N_LEVELS - 1
    a0 = [jnp.where(diag, _dot_nt(qs[h][:H].astype(BF16), ks[h][:H].astype(BF16)), 0.0) for h in heads]
    a1 = [jnp.where(diag, _dot_nt(qs[h][H:].astype(BF16), ks[h][H:].astype(BF16)), 0.0) for h in heads]
    a_off = [None] * HG_HEADS
    zs = list(bs)
    for lev in range(N_LEVELS):
        m = 1 << lev
        mask = lv == lev
        q_side = ((row >> lev) & 1) == far
        for h in heads:
            bh, qh, kh = bs[h], qs[h], ks[h]
            if lev < N_SMALL_LEVELS:
                bmid = _replace_bit(zs[h], row, lev, near)
                zs[h] = _replace_bit(zs[h], row, lev, far)
                qk = jnp.where(q_side, qh, kh)
                arg = _neg_abs(bh - bmid)
            elif 2 * m == SUBLANES:
                ref = m if reverse else m - 1
                bmid = jnp.concatenate([jnp.broadcast_to(bh[lo + ref:lo + ref + 1], (SUBLANES, HG_DK))
                                        for lo in range(0, R, SUBLANES)], axis=0)
                qk = jnp.where(q_side, qh, kh)
                arg = _neg_abs(bh - bmid)
            else:
                args, parts = [], []
                for j in range(R // (2 * m)):
                    lo = j * 2 * m
                    ref = lo + (m if reverse else m - 1)
                    bref = bh[ref:ref + 1]
                    if reverse:
                        args += [bh[lo:lo + m] - bref, bref - bh[lo + m:lo + 2 * m]]
                        parts += [qh[lo:lo + m], kh[lo + m:lo + 2 * m]]
                    else:
                        args += [bref - bh[lo:lo + m], bh[lo + m:lo + 2 * m] - bref]
                        parts += [kh[lo:lo + m], qh[lo + m:lo + 2 * m]]
                arg = jnp.concatenate(args, axis=0)
                qk = jnp.concatenate(parts, axis=0)
            u = (qk * jnp.exp2(arg)).astype(BF16)
            if lev < N_LEVELS - 1:
                a0[h] = jnp.where(mask, _dot_nt(u[:H], u[:H]), a0[h])
                a1[h] = jnp.where(mask, _dot_nt(u[H:], u[H:]), a1[h])
            elif reverse:
                a_off[h] = _dot_nt(u[:H], u[H:])
            else:
                a_off[h] = _dot_nt(u[H:], u[:H])

    for h in heads:
        sl, bh, qh, kh, vh = sls[h], bs[h], qs[h], ks[h], vs[h]
        zend = bh[end_row:end_row + 1]
        st = st_ref[h]
        if reverse:
            o_in = jnp.concatenate([
                _dot(jnp.concatenate([a0[h], a_off[h]], axis=1).astype(BF16), vh),
                _dot(a1[h].astype(BF16), vh[H:])], axis=0)
        else:
            o_in = jnp.concatenate([
                _dot(a0[h].astype(BF16), vh[:H]),
                _dot(jnp.concatenate([a_off[h], a1[h]], axis=1).astype(BF16), vh)], axis=0)
        o = o_in + _dot_nt((qh * jnp.exp2(bh)).astype(BF16), st.astype(BF16))
        k_end = (kh * jnp.exp2(zend - bh)).astype(BF16)
        st_ref[h] = jnp.exp2(zend) * st + _dot_tn(vh, k_end)

        if reverse:
            tot = of_ref[0, :, sl] + o
            out_ref[0, :, sl] = (_rms(tot, hgn_ref[...]) * g_ref[0, :, sl].astype(F32)).astype(out_ref.dtype)
        else:
            out_ref[0, :, sl] = o


def _hgrn_call(reverse, lat, ctx, tri, lv, extra=()):
    B, T, KW = lat[0].shape
    R = ROW_TILE
    n_lat = T // R

    if reverse:
        def lat_idx(b, i):
            return (b, jnp.where(i == 0, n_lat - 1, n_lat - i), 0)
    else:
        def lat_idx(b, i):
            return (b, jnp.maximum(i - 1, 0), 0)

    const2 = lambda b, i: (0, 0)
    in_specs = [pl.BlockSpec((1, R, KW), lat_idx)] * 3 + [pl.BlockSpec((1, R, KW), lambda b, i: (b, 0, 0))] * 3 + [
        pl.BlockSpec((R, R), const2),
        pl.BlockSpec((R // 2, R // 2), const2),
    ]
    if reverse:
        in_specs += [
            pl.BlockSpec((1, R, KW), lat_idx),
            pl.BlockSpec((1, R, KW), lat_idx),
            pl.BlockSpec((1, HG_DV), const2),
        ]
        out_dtype = BF16
    else:
        out_dtype = F32
    return pl.pallas_call(
        functools.partial(_hgrn_kernel, reverse),
        grid=(B, n_lat + 1),
        in_specs=in_specs,
        out_specs=pl.BlockSpec((1, R, KW), lat_idx),
        out_shape=jax.ShapeDtypeStruct((B, T, KW), out_dtype),
        scratch_shapes=[pltpu.VMEM((HG_HEADS, HG_DV, HG_DK), F32)],
        compiler_params=pltpu.CompilerParams(dimension_semantics=("arbitrary", "arbitrary"),
                                             vmem_limit_bytes=VMEM_LIMIT),
        name="hgrn_bwd" if reverse else "hgrn_fwd",
    )(*lat, *ctx, tri, lv, *extra)


def _attn_kernel(q_ref, k_ref, kc_ref, v_ref, vc_ref, o_ref):
    n_lat = k_ref.shape[1]

    def scores(h):
        sl = slice(h * HEAD_SLOT, (h + 1) * HEAD_SLOT)
        qh = q_ref[0, :, sl]
        return jnp.concatenate([_dot_nt(qh, k_ref[0, :, sl]), _dot_nt(qh, kc_ref[0, :, sl])], axis=1)

    def attend(h, s):
        vl = slice((h // 2) * LANES, (h // 2 + 1) * LANES)
        p = jnp.exp2(s - jnp.max(s, axis=-1, keepdims=True))
        l = jnp.sum(p, axis=-1, keepdims=True)
        pb = p.astype(BF16)
        o = _dot(pb[:, :n_lat], v_ref[0, :, vl]) + _dot(pb[:, n_lat:], vc_ref[0, :, vl])
        return o * (1.0 / l)

    outs = []
    s_next = scores(0)
    for h in range(MLA_HEADS):
        s = s_next
        if h + 1 < MLA_HEADS:
            s_next = scores(h + 1)
        outs.append(attend(h, s))
    first = _lane_lt(q_ref.shape[1], MLA_V)
    pairs = [jnp.where(first, outs[2 * j], outs[2 * j + 1]) for j in range(MLA_HEADS // 2)]
    o_ref[0] = jnp.concatenate(pairs, axis=-1).astype(o_ref.dtype)


def _attn_call(q, k, k_ctx, v, v_ctx):
    B, T, nq = q.shape
    R = ATTN_ROW_TILE
    whole = lambda a: pl.BlockSpec((1,) + a.shape[1:], lambda b, i: (b, 0, 0))
    return pl.pallas_call(
        _attn_kernel,
        grid=(B, T // R),
        in_specs=[pl.BlockSpec((1, R, nq), lambda b, i: (b, i, 0)), whole(k), whole(k_ctx), whole(v), whole(v_ctx)],
        out_specs=pl.BlockSpec((1, R, MLA_HEADS * MLA_V), lambda b, i: (b, i, 0)),
        out_shape=jax.ShapeDtypeStruct((B, T, MLA_HEADS * MLA_V), BF16),
        compiler_params=pltpu.CompilerParams(dimension_semantics=("arbitrary", "arbitrary"),
                                             vmem_limit_bytes=VMEM_LIMIT),
        name="mla_attn",
    )(q, k, k_ctx, v, v_ctx)


def _mlp_kernel(n_ff_chunks, x_ref, om_ref, oh_ref, mod_ref, woa_ref, wob_ref, nmlp_ref, w1_ref, w2_ref,
                fn_ref, o_ref):
    mod = mod_ref[0]
    y = _dot(om_ref[0], woa_ref[...]) + _dot(oh_ref[0], wob_ref[...])
    x1 = x_ref[0] + mod[0:1] * y
    h2 = (_rms(x1, nmlp_ref[...]) * (1.0 + mod[2:3]) + mod[1:2]).astype(BF16)
    ff = w1_ref.shape[1] // n_ff_chunks
    acc = jnp.zeros(x1.shape, F32)
    for c in range(n_ff_chunks):
        u = jnp.maximum(_dot(h2, w1_ref[:, c * ff:(c + 1) * ff]), 0.0)
        acc = acc + _dot((u * u).astype(BF16), w2_ref[c * ff:(c + 1) * ff, :])
    x2 = x1 + mod[3:4] * acc
    o_ref[0] = _rms(x2, fn_ref[...])


def _mlp_call(x, o_mla, o_hg, mod4, woa, wob, nmlp, w1, w2, fnorm):
    B, T, D = x.shape
    R = MLP_ROW_TILE
    dff = w1.shape[1]
    half = o_mla.shape[2]
    row_idx = lambda b, i: (b, i, 0)
    const2 = lambda b, i: (0, 0)
    return pl.pallas_call(
        functools.partial(_mlp_kernel, 4),
        grid=(B, T // R),
        in_specs=[
            pl.BlockSpec((1, R, D), row_idx),
            pl.BlockSpec((1, R, half), row_idx),
            pl.BlockSpec((1, R, half), row_idx),
            pl.BlockSpec((1, 4, D), lambda b, i: (b, 0, 0)),
            pl.BlockSpec((half, D), const2, pipeline_mode=pl.Buffered(1)),
            pl.BlockSpec((half, D), const2, pipeline_mode=pl.Buffered(1)),
            pl.BlockSpec((1, D), const2),
            pl.BlockSpec((D, dff), const2, pipeline_mode=pl.Buffered(1)),
            pl.BlockSpec((dff, D), const2, pipeline_mode=pl.Buffered(1)),
            pl.BlockSpec((1, D), const2),
        ],
        out_specs=pl.BlockSpec((1, R, D), row_idx),
        out_shape=jax.ShapeDtypeStruct((B, T, D), F32),
        compiler_params=pltpu.CompilerParams(dimension_semantics=("arbitrary", "arbitrary"),
                                             vmem_limit_bytes=VMEM_LIMIT),
        name="out_mlp",
    )(x, o_mla, o_hg, mod4, woa, wob, nmlp, w1, w2, fnorm)


def _rope_slot(w):
    z = jnp.zeros(w.shape[:-1] + (MLA_NOPE,), w.dtype)
    z2 = jnp.zeros(w.shape[:-1] + (HEAD_SLOT - MLA_NOPE - MLA_ROPE,), w.dtype)
    return jnp.concatenate([z, w, z2], axis=-1)


def _pack_w_in(w_in):
    o = 0
    parts = []
    for n in (MLA_Q_RANK, MLA_KV_RANK, MLA_ROPE, HG_KW, HG_KW, HG_KW, HG_KW, HG_KW):
        parts.append(w_in[:, o:o + n])
        o += n
    cq, ckv, kr, hq, hf, hb, hi, hg = parts
    return jnp.concatenate([cq, ckv, _rope_slot(kr), hq, hf, hb, hi, hg], axis=1).astype(BF16)


def _pack_w_uq(w_uq):
    k = w_uq.shape[0]
    w3 = w_uq.reshape(k, MLA_HEADS, MLA_NOPE + MLA_ROPE)
    nope = w3[..., :MLA_NOPE].reshape(k, MLA_HEADS * MLA_NOPE)
    rope = w3[..., MLA_NOPE:].reshape(k, MLA_HEADS // 2, 2 * MLA_ROPE)
    rope = jnp.concatenate([jnp.zeros((k, MLA_HEADS // 2, LANES - 2 * MLA_ROPE), w_uq.dtype), rope], axis=-1)
    return jnp.concatenate([nope, rope.reshape(k, (MLA_HEADS // 2) * LANES)], axis=1).astype(BF16)


def _pack_w_ukv(w_ukv):
    k = w_ukv.shape[0]
    w3 = w_ukv.reshape(k, MLA_HEADS, MLA_NOPE + MLA_V)
    kk = w3[..., :MLA_NOPE].reshape(k, MLA_HEADS * MLA_NOPE)
    vv = w3[..., MLA_NOPE:].reshape(k, MLA_HEADS * MLA_V)
    return jnp.concatenate([kk, vv], axis=1).astype(BF16)


def _rope_tables(T):
    rows = T // GRID_W
    row = np.repeat(np.arange(rows), GRID_W).astype(np.float64)
    col = np.tile(np.arange(GRID_W), rows).astype(np.float64)
    n = MLA_ROPE // 4
    inv = ROPE_BASE ** (-np.arange(n, dtype=np.float64) / n)
    ar, ac = row[:, None] * inv, col[:, None] * inv
    cos32 = np.concatenate([np.cos(ar), np.cos(ar), np.cos(ac), np.cos(ac)], axis=-1)
    sin32 = np.concatenate([np.sin(ar), np.sin(ar), np.sin(ac), np.sin(ac)], axis=-1)
    cos_t = np.concatenate([np.ones((T, MLA_NOPE)), cos32, cos32], axis=-1)
    sin_t = np.concatenate([np.zeros((T, MLA_NOPE)), sin32, sin32], axis=-1)
    return jnp.asarray(cos_t, F32), jnp.asarray(sin_t, F32)


def _scan_constants(reverse):
    R = ROW_TILE
    t = np.arange(R, dtype=np.int32)[:, None]
    s = np.arange(R, dtype=np.int32)[None, :]
    if reverse:
        t, s = s, t
    tri = (t >= s)
    H = R // 2
    t, s = np.broadcast_to(t, (R, R))[:H, :H], np.broadcast_to(s, (R, R))[:H, :H]
    x = t ^ s
    lev = np.zeros((H, H), np.int32)
    for bit in range(1, N_LEVELS - 1):
        lev = np.where((x >> bit) > 0, bit, lev)
    lv = np.where(t > s, lev, np.where(t == s, N_LEVELS - 1, -1))
    return jnp.asarray(tri, BF16), jnp.asarray(lv, jnp.int32)


def kernel(x, c, ctx, c_ctx, w_ada, b_ada, norm_mix, w_in, q_norm, w_uq, kv_norm, w_ukv, hgrn_lb, hgrn_norm,
           w_out, norm_mlp, w_mlp_in, w_mlp_out, final_norm):
    B, T, D = x.shape
    L = ctx.shape[1]
    assert w_ada.shape[0] == 1, "single-layer block"
    assert T % ROW_TILE == 0 and L == ROW_TILE and T % GRID_W == 0
    assert T % MLP_ROW_TILE == 0 and T % ATTN_ROW_TILE == 0 and T % PROJ_ROW_TILE == 0
    assert PROJ_ROW_TILE % L == 0 and B % (PROJ_ROW_TILE // L) == 0 and T % L == 0

    pad_rows = (-(B + 1)) % 8
    cc = jnp.concatenate([c, c_ctx[None, :], jnp.zeros((pad_rows, D), F32)], axis=0)
    mod = _ada_call(cc, w_ada[0], b_ada[0][None, :])
    mod_lat = mod[:B].reshape(B, 6, D)
    mod_ctx = mod[B].reshape(1, 6, D)
    mod4 = mod_lat[:, 2:6]

    cos_t, sin_t = _rope_tables(T)
    win_p, wkv_p = _pack_w_in(w_in[0]), _pack_w_ukv(w_ukv[0])
    q, k, v, qh, lff, lfb, vi, g = _inproj_lat_call(
        x, mod_lat[:, 0:2], norm_mix, win_p, q_norm, _pack_w_uq(w_uq[0]), kv_norm, wkv_p, hgrn_lb, cos_t, sin_t)
    kc, vc, qhc, lffc, lfbc, vic = _inproj_ctx_call(
        ctx, mod_ctx[:, 0:2], norm_mix, win_p, kv_norm, wkv_p, hgrn_lb)

    tri_f, lv_f = _scan_constants(False)
    tri_b, lv_b = _scan_constants(True)
    o_f = _hgrn_call(False, (qh, lff, vi), (qhc, lffc, vic), tri_f, lv_f)
    o_hg = _hgrn_call(True, (qh, lfb, vi), (qhc, lfbc, vic), tri_b, lv_b, extra=(o_f, g, hgrn_norm))

    o_mla = _attn_call(q, k, kc, v, vc)

    half = MLA_HEADS * MLA_V
    wo = w_out[0].astype(BF16)
    return _mlp_call(x, o_mla, o_hg, mod4, wo[:half], wo[half:], norm_mlp, w_mlp_in[0].astype(BF16),
                     w_mlp_out[0].astype(BF16), final_norm[None, :])
```

```python
import functools

import jax
import jax.numpy as jnp
import numpy as np
from jax import lax
from jax.experimental import pallas as pl
from jax.experimental.pallas import tpu as pltpu

F32 = jnp.float32
BF16 = jnp.bfloat16

GRID_W = 64
MLA_HEADS = 8
MLA_Q_RANK = 256
MLA_KV_RANK = 128
MLA_NOPE = 64
MLA_ROPE = 32
MLA_V = 64
HG_HEADS = 4
HG_DK = 128
HG_DV = 128
HG_KW = HG_HEADS * HG_DK
ROPE_BASE = 10000.0
EPS = 1e-6
LOG2E = 1.4426950408889634

LANES = 128
SUBLANES = 8
HEAD_SLOT = LANES
ROW_TILE = 256
HGRN_SAMPLES = 2
MLP_ROW_TILE = 512
PROJ_ROW_TILE = 512
ATTN_ROW_TILE = 512
VMEM_LIMIT = 56 * 1024 * 1024

_C_CQ = (0, 256)
_C_CKV = (256, 384)
_C_KR = (384, 512)
_C_HQ = (512, 1024)
_C_HF = (1024, 1536)
_C_HB = (1536, 2048)
_C_HI = (2048, 2560)
_C_HG = (2560, 3072)
IN_P_WIDTH = 3072


def _rms(x, g):
    ms = jnp.mean(x * x, axis=-1, keepdims=True)
    return (x * lax.rsqrt(ms + EPS)) * g


def _silu(x):
    return x * jax.nn.sigmoid(x)


def _dot(a, b):
    return jnp.dot(a, b, preferred_element_type=F32)


def _dot_nt(a, b):
    return lax.dot_general(a, b, (((1,), (1,)), ((), ())), preferred_element_type=F32)


def _dot_tn(a, b):
    return lax.dot_general(a, b, (((0,), (0,)), ((), ())), preferred_element_type=F32)


def _ada_kernel(c_ref, w_ref, b_ref, o_ref):
    s = _silu(c_ref[...])
    o_ref[...] = jnp.dot(s, w_ref[...], preferred_element_type=F32,
                         precision=lax.Precision.HIGHEST) + b_ref[...]


def _ada_call(cc, w_ada, b_ada):
    rows, d = cc.shape
    n = w_ada.shape[1]
    tn = 1024
    return pl.pallas_call(
        _ada_kernel,
        grid=(n // tn,),
        in_specs=[
            pl.BlockSpec((rows, d), lambda j: (0, 0)),
            pl.BlockSpec((d, tn), lambda j: (0, j)),
            pl.BlockSpec((1, tn), lambda j: (0, j)),
        ],
        out_specs=pl.BlockSpec((rows, tn), lambda j: (0, j)),
        out_shape=jax.ShapeDtypeStruct((rows, n), F32),
        compiler_params=pltpu.CompilerParams(dimension_semantics=("arbitrary",), vmem_limit_bytes=VMEM_LIMIT),
        name="ada_mod",
    )(cc, w_ada, b_ada)


def _lane_lt(n_rows, bound):
    return lax.broadcasted_iota(jnp.int32, (n_rows, LANES), 1) < bound


def _rotate_half_lanes(x):
    lane = lax.broadcasted_iota(jnp.int32, x.shape, 1)
    return jnp.where((lane & 15) < 8, -pltpu.roll(x, LANES - 8, 1), pltpu.roll(x, 8, 1))


def _head_slots(nope, rope_lo, rope_hi):
    first = _lane_lt(nope.shape[0], MLA_NOPE)
    return (jnp.where(first, nope, rope_lo), jnp.where(first, pltpu.roll(nope, MLA_NOPE, 1), rope_hi))


def _inproj_kernel(is_ctx, *refs):
    if is_ctx:
        (x_ref, mod_ref, nmix_ref, win_ref, kvn_ref, wkv_ref, lb_ref,
         k_out, v_out, qh_out, lff_out, lfb_out, vi_out) = refs
    else:
        (x_ref, mod_ref, nmix_ref, win_ref, qn_ref, wq_ref, kvn_ref, wkv_ref, lb_ref, cos_ref, sin_ref,
         q_out, k_out, v_out, qh_out, lff_out, lfb_out, vi_out, g_out) = refs
    mod = mod_ref[0]
    h = _rms(x_ref[0], nmix_ref[...]) * (1.0 + mod[1:2]) + mod[0:1]
    hb = h.astype(BF16)
    n_pairs = MLA_HEADS // 2
    half = n_pairs * LANES

    def proj(c):
        return _dot(hb, win_ref[:, c[0]:c[1]])

    def put(ref, val):
        ref[...] = val.astype(ref.dtype).reshape(ref.shape)

    def pair(a, j):
        return a[:, j * LANES:(j + 1) * LANES]

    lbr = lb_ref[...]
    e = jnp.exp(lbr - jnp.max(lbr, axis=0, keepdims=True))
    lb = e[0] / jnp.sum(e, axis=0)
    lbf = lb[0:1]
    lbb = lb[1:2]

    ckv = proj(_C_CKV)
    kr = proj(_C_KR)
    if not is_ctx:
        cq = proj(_C_CQ)
    put(lff_out, jnp.log(lbf + (1.0 - lbf) * jax.nn.sigmoid(proj(_C_HF))))
    put(vi_out, proj(_C_HI))
    kv = _dot(_rms(ckv, kvn_ref[...]).astype(BF16), wkv_ref[...])
    put(lfb_out, jnp.log(lbb + (1.0 - lbb) * jax.nn.sigmoid(proj(_C_HB))))
    if not is_ctx:
        cos = cos_ref[...]
        sin = sin_ref[...]
        kr = kr * cos + _rotate_half_lanes(kr) * sin
        qq = _dot(_rms(cq, qn_ref[...]).astype(BF16), wq_ref[...])
    put(qh_out, _silu(proj(_C_HQ)))
    k_slots = []
    for j in range(n_pairs):
        k_slots += _head_slots(pair(kv, j), kr, kr)
    put(k_out, jnp.concatenate(k_slots, axis=1))
    put(v_out, kv[:, half:])
    if not is_ctx:
        scale = (MLA_NOPE + MLA_ROPE) ** -0.5 * LOG2E
        q_slots = []
        for j in range(n_pairs):
            r = pair(qq, n_pairs + j)
            r = (r * cos + _rotate_half_lanes(r) * sin) * scale
            q_slots += _head_slots(pair(qq, j) * scale, r, pltpu.roll(r, LANES - MLA_ROPE, 1))
        put(q_out, jnp.concatenate(q_slots, axis=1))
        put(g_out, _silu(proj(_C_HG)))


def _inproj_out_shapes(B, rows, latent):
    nq = MLA_HEADS * HEAD_SLOT
    nv = MLA_HEADS * MLA_V
    shapes = [
        ((B, rows, nq), BF16),
        ((B, rows, nv), BF16),
        ((B, rows, HG_KW), BF16),
        ((B, rows, HG_KW), F32),
        ((B, rows, HG_KW), F32),
        ((B, rows, HG_KW), BF16),
    ]
    if latent:
        shapes = [((B, rows, nq), BF16)] + shapes + [((B, rows, HG_KW), BF16)]
    return shapes


def _inproj_lat_call(x, mod_lat, nmix, win_p, qn, wq_p, kvn, wkv_p, lb_raw, cos_t, sin_t):
    B, T, D = x.shape
    R = PROJ_ROW_TILE
    nq = MLA_HEADS * HEAD_SLOT
    row_idx = lambda b, i: (b, i, 0)
    const2 = lambda b, i: (0, 0)
    in_specs = [
        pl.BlockSpec((1, R, D), row_idx),
        pl.BlockSpec((1, 2, D), lambda b, i: (b, 0, 0)),
        pl.BlockSpec((1, D), const2),
        pl.BlockSpec((D, IN_P_WIDTH), const2, pipeline_mode=pl.Buffered(1)),
        pl.BlockSpec((1, MLA_Q_RANK), const2),
        pl.BlockSpec((MLA_Q_RANK, nq), const2, pipeline_mode=pl.Buffered(1)),
        pl.BlockSpec((1, MLA_KV_RANK), const2),
        pl.BlockSpec((MLA_KV_RANK, nq), const2, pipeline_mode=pl.Buffered(1)),
        pl.BlockSpec(lb_raw.shape, lambda b, i: (0, 0, 0)),
        pl.BlockSpec((R, LANES), lambda b, i: (i, 0)),
        pl.BlockSpec((R, LANES), lambda b, i: (i, 0)),
    ]
    shapes = _inproj_out_shapes(B, T, True)
    return pl.pallas_call(
        functools.partial(_inproj_kernel, False),
        grid=(B, T // R),
        in_specs=in_specs,
        out_specs=[pl.BlockSpec((1, R, s[2]), row_idx) for s, _ in shapes],
        out_shape=[jax.ShapeDtypeStruct(s, d) for s, d in shapes],
        compiler_params=pltpu.CompilerParams(dimension_semantics=("arbitrary", "arbitrary"),
                                             vmem_limit_bytes=VMEM_LIMIT),
        name="inproj_lat",
    )(x, mod_lat, nmix, win_p, qn, wq_p, kvn, wkv_p, lb_raw, cos_t, sin_t)


def _inproj_ctx_call(ctx, mod_ctx, nmix, win_p, kvn, wkv_p, lb_raw):
    B, L, D = ctx.shape
    R = PROJ_ROW_TILE
    per = R // L
    nq = MLA_HEADS * HEAD_SLOT
    x2 = ctx.reshape(B // per, R, D)
    const2 = lambda i: (0, 0)
    in_specs = [
        pl.BlockSpec((1, R, D), lambda i: (i, 0, 0)),
        pl.BlockSpec((1, 2, D), lambda i: (0, 0, 0)),
        pl.BlockSpec((1, D), const2),
        pl.BlockSpec((D, IN_P_WIDTH), const2, pipeline_mode=pl.Buffered(1)),
        pl.BlockSpec((1, MLA_KV_RANK), const2),
        pl.BlockSpec((MLA_KV_RANK, nq), const2, pipeline_mode=pl.Buffered(1)),
        pl.BlockSpec(lb_raw.shape, lambda i: (0, 0, 0)),
    ]
    shapes = _inproj_out_shapes(B, L, False)
    return pl.pallas_call(
        functools.partial(_inproj_kernel, True),
        grid=(B // per,),
        in_specs=in_specs,
        out_specs=[pl.BlockSpec((per, L, s[2]), lambda i: (i, 0, 0)) for s, _ in shapes],
        out_shape=[jax.ShapeDtypeStruct(s, d) for s, d in shapes],
        compiler_params=pltpu.CompilerParams(dimension_semantics=("arbitrary",), vmem_limit_bytes=VMEM_LIMIT),
        name="inproj_ctx",
    )(x2, mod_ctx, nmix, win_p, kvn, wkv_p, lb_raw)


N_LEVELS = 8
N_SMALL_LEVELS = 2


def _neg_abs(x):
    bits = lax.bitcast_convert_type(x, jnp.uint32) | jnp.uint32(0x80000000)
    return lax.bitcast_convert_type(bits, F32)


def _replace_bit(y, row, bit, value):
    step = 1 << bit
    n = y.shape[0]
    has = ((row >> bit) & 1) == 1
    if value == 1:
        return jnp.where(has, y, pltpu.roll(y, n - step, 0))
    return jnp.where(has, pltpu.roll(y, step, 0), y)


def _hgrn_kernel(reverse, *refs):
    st_ref = refs[-1]
    i = pl.program_id(1)

    @pl.when(i == 0)
    def _():
        st_ref[...] = jnp.zeros_like(st_ref)

    stages = [_hgrn_sample(reverse, n, i == 0, *refs) for n in range(refs[0].shape[0])]
    while stages:
        stages = [s for s in stages if next(s, False)]


def _hgrn_sample(reverse, n, is_ctx, qh_ref, lf_ref, vi_ref, qhc_ref, lfc_ref, vic_ref, tri_ref, lv_ref, *rest):
    if reverse:
        of_ref, g_ref, hgn_ref, out_ref, st_ref = rest
    else:
        out_ref, st_ref = rest
    lf = jnp.where(is_ctx, lfc_ref[n], lf_ref[n])
    qh_in = jnp.where(is_ctx, qhc_ref[n], qh_ref[n])
    vi_in = jnp.where(is_ctx, vic_ref[n], vi_ref[n])
    l_hi = lf.astype(BF16)
    r1 = lf - l_hi.astype(F32)
    l_mid = r1.astype(BF16)
    l_lo = (r1 - l_mid.astype(F32)).astype(BF16)
    tri = tri_ref[...]
    b_all = (_dot(tri, l_hi) + _dot(tri, l_mid) + _dot(tri, l_lo)) * LOG2E
    k_all = 1.0 - jnp.exp(lf)
    q_all = qh_in.astype(F32)
    lv = lv_ref[...]
    R = lf.shape[0]
    H = R // 2
    row = lax.broadcasted_iota(jnp.int32, (R, LANES), 0)
    near, far = (0, 1) if not reverse else (1, 0)
    end_row = 0 if reverse else R - 1
    heads = range(HG_HEADS)
    sls = [slice(h * HG_DK, (h + 1) * HG_DK) for h in heads]
    bs = [b_all[:, sl] for sl in sls]
    qs = [q_all[:, sl] for sl in sls]
    ks = [k_all[:, sl] for sl in sls]
    vs = [vi_in[:, sl] for sl in sls]

    diag = lv == N_LEVELS - 1
    a0 = [jnp.where(diag, _dot_nt(qs[h][:H].astype(BF16), ks[h][:H].astype(BF16)), 0.0) for h in heads]
    a1 = [jnp.where(diag, _dot_nt(qs[h][H:].astype(BF16), ks[h][H:].astype(BF16)), 0.0) for h in heads]
    a_off = [None] * HG_HEADS
    zs = list(bs)
    for lev in range(N_LEVELS):
        m = 1 << lev
        mask = lv == lev
        q_side = ((row >> lev) & 1) == far
        for h in heads:
            bh, qh, kh = bs[h], qs[h], ks[h]
            if lev < N_SMALL_LEVELS:
                bmid = _replace_bit(zs[h], row, lev, near)
                zs[h] = _replace_bit(zs[h], row, lev, far)
                qk = jnp.where(q_side, qh, kh)
                arg = _neg_abs(bh - bmid)
            elif 2 * m == SUBLANES:
                ref = m if reverse else m - 1
                bmid = jnp.concatenate([jnp.broadcast_to(bh[lo + ref:lo + ref + 1], (SUBLANES, HG_DK))
                                        for lo in range(0, R, SUBLANES)], axis=0)
                qk = jnp.where(q_side, qh, kh)
                arg = _neg_abs(bh - bmid)
            else:
                args, parts = [], []
                for j in range(R // (2 * m)):
                    lo = j * 2 * m
                    ref = lo + (m if reverse else m - 1)
                    bref = bh[ref:ref + 1]
                    if reverse:
                        args += [bh[lo:lo + m] - bref, bref - bh[lo + m:lo + 2 * m]]
                        parts += [qh[lo:lo + m], kh[lo + m:lo + 2 * m]]
                    else:
                        args += [bref - bh[lo:lo + m], bh[lo + m:lo + 2 * m] - bref]
                        parts += [kh[lo:lo + m], qh[lo + m:lo + 2 * m]]
                arg = jnp.concatenate(args, axis=0)
                qk = jnp.concatenate(parts, axis=0)
            u = (qk * jnp.exp2(arg)).astype(BF16)
            if lev < N_LEVELS - 1:
                a0[h] = jnp.where(mask, _dot_nt(u[:H], u[:H]), a0[h])
                a1[h] = jnp.where(mask, _dot_nt(u[H:], u[H:]), a1[h])
            elif reverse:
                a_off[h] = _dot_nt(u[:H], u[H:])
            else:
                a_off[h] = _dot_nt(u[H:], u[:H])
        yield True

    for h in heads:
        sl, bh, qh, kh, vh = sls[h], bs[h], qs[h], ks[h], vs[h]
        zend = bh[end_row:end_row + 1]
        st = st_ref[n * HG_HEADS + h]
        if reverse:
            o_in = jnp.concatenate([
                _dot(jnp.concatenate([a0[h], a_off[h]], axis=1).astype(BF16), vh),
                _dot(a1[h].astype(BF16), vh[H:])], axis=0)
        else:
            o_in = jnp.concatenate([
                _dot(a0[h].astype(BF16), vh[:H]),
                _dot(jnp.concatenate([a_off[h], a1[h]], axis=1).astype(BF16), vh)], axis=0)
        o = o_in + _dot_nt((qh * jnp.exp2(bh)).astype(BF16), st.astype(BF16))
        k_end = (kh * jnp.exp2(zend - bh)).astype(BF16)
        st_ref[n * HG_HEADS + h] = jnp.exp2(zend) * st + _dot_tn(vh, k_end)

        if reverse:
            tot = of_ref[n, :, sl] + o
            out_ref[n, :, sl] = (_rms(tot, hgn_ref[...]) * g_ref[n, :, sl].astype(F32)).astype(out_ref.dtype)
        else:
            out_ref[n, :, sl] = o
        yield True


def _hgrn_call(reverse, lat, ctx, tri, lv, extra=()):
    B, T, KW = lat[0].shape
    R = ROW_TILE
    n_lat = T // R

    if reverse:
        def lat_idx(b, i):
            return (b, jnp.where(i == 0, n_lat - 1, n_lat - i), 0)
    else:
        def lat_idx(b, i):
            return (b, jnp.maximum(i - 1, 0), 0)

    NS = HGRN_SAMPLES
    const2 = lambda b, i: (0, 0)
    in_specs = [pl.BlockSpec((NS, R, KW), lat_idx)] * 3 + [pl.BlockSpec((NS, R, KW), lambda b, i: (b, 0, 0))] * 3 + [
        pl.BlockSpec((R, R), const2),
        pl.BlockSpec((R // 2, R // 2), const2),
    ]
    if reverse:
        in_specs += [
            pl.BlockSpec((NS, R, KW), lat_idx),
            pl.BlockSpec((NS, R, KW), lat_idx),
            pl.BlockSpec((1, HG_DV), const2),
        ]
        out_dtype = BF16
    else:
        out_dtype = F32
    return pl.pallas_call(
        functools.partial(_hgrn_kernel, reverse),
        grid=(B // NS, n_lat + 1),
        in_specs=in_specs,
        out_specs=pl.BlockSpec((NS, R, KW), lat_idx),
        out_shape=jax.ShapeDtypeStruct((B, T, KW), out_dtype),
        scratch_shapes=[pltpu.VMEM((NS * HG_HEADS, HG_DV, HG_DK), F32)],
        compiler_params=pltpu.CompilerParams(dimension_semantics=("arbitrary", "arbitrary"),
                                             vmem_limit_bytes=VMEM_LIMIT),
        name="hgrn_bwd" if reverse else "hgrn_fwd",
    )(*lat, *ctx, tri, lv, *extra)


def _attn_kernel(q_ref, k_ref, kc_ref, v_ref, vc_ref, o_ref):
    n_lat = k_ref.shape[1]

    def scores(h):
        sl = slice(h * HEAD_SLOT, (h + 1) * HEAD_SLOT)
        qh = q_ref[0, :, sl]
        return jnp.concatenate([_dot_nt(qh, k_ref[0, :, sl]), _dot_nt(qh, kc_ref[0, :, sl])], axis=1)

    def attend(h, s):
        vl = slice((h // 2) * LANES, (h // 2 + 1) * LANES)
        p = jnp.exp2(s - jnp.max(s, axis=-1, keepdims=True))
        l = jnp.sum(p, axis=-1, keepdims=True)
        pb = p.astype(BF16)
        o = _dot(pb[:, :n_lat], v_ref[0, :, vl]) + _dot(pb[:, n_lat:], vc_ref[0, :, vl])
        return o * (1.0 / l)

    outs = []
    s_next = scores(0)
    for h in range(MLA_HEADS):
        s = s_next
        if h + 1 < MLA_HEADS:
            s_next = scores(h + 1)
        outs.append(attend(h, s))
    first = _lane_lt(q_ref.shape[1], MLA_V)
    pairs = [jnp.where(first, outs[2 * j], outs[2 * j + 1]) for j in range(MLA_HEADS // 2)]
    o_ref[0] = jnp.concatenate(pairs, axis=-1).astype(o_ref.dtype)


def _attn_call(q, k, k_ctx, v, v_ctx):
    B, T, nq = q.shape
    R = ATTN_ROW_TILE
    whole = lambda a: pl.BlockSpec((1,) + a.shape[1:], lambda b, i: (b, 0, 0))
    return pl.pallas_call(
        _attn_kernel,
        grid=(B, T // R),
        in_specs=[pl.BlockSpec((1, R, nq), lambda b, i: (b, i, 0)), whole(k), whole(k_ctx), whole(v), whole(v_ctx)],
        out_specs=pl.BlockSpec((1, R, MLA_HEADS * MLA_V), lambda b, i: (b, i, 0)),
        out_shape=jax.ShapeDtypeStruct((B, T, MLA_HEADS * MLA_V), BF16),
        compiler_params=pltpu.CompilerParams(dimension_semantics=("arbitrary", "arbitrary"),
                                             vmem_limit_bytes=VMEM_LIMIT),
        name="mla_attn",
    )(q, k, k_ctx, v, v_ctx)


def _mlp_kernel(n_ff_chunks, x_ref, om_ref, oh_ref, mod_ref, woa_ref, wob_ref, nmlp_ref, w1_ref, w2_ref,
                fn_ref, o_ref):
    mod = mod_ref[0]
    y = _dot(om_ref[0], woa_ref[...]) + _dot(oh_ref[0], wob_ref[...])
    x1 = x_ref[0] + mod[0:1] * y
    h2 = (_rms(x1, nmlp_ref[...]) * (1.0 + mod[2:3]) + mod[1:2]).astype(BF16)
    ff = w1_ref.shape[1] // n_ff_chunks
    acc = jnp.zeros(x1.shape, F32)
    for c in range(n_ff_chunks):
        u = jnp.maximum(_dot(h2, w1_ref[:, c * ff:(c + 1) * ff]), 0.0)
        acc = acc + _dot((u * u).astype(BF16), w2_ref[c * ff:(c + 1) * ff, :])
    x2 = x1 + mod[3:4] * acc
    o_ref[0] = _rms(x2, fn_ref[...])


def _mlp_call(x, o_mla, o_hg, mod4, woa, wob, nmlp, w1, w2, fnorm):
    B, T, D = x.shape
    R = MLP_ROW_TILE
    dff = w1.shape[1]
    half = o_mla.shape[2]
    row_idx = lambda b, i: (b, i, 0)
    const2 = lambda b, i: (0, 0)
    return pl.pallas_call(
        functools.partial(_mlp_kernel, 4),
        grid=(B, T // R),
        in_specs=[
            pl.BlockSpec((1, R, D), row_idx),
            pl.BlockSpec((1, R, half), row_idx),
            pl.BlockSpec((1, R, half), row_idx),
            pl.BlockSpec((1, 4, D), lambda b, i: (b, 0, 0)),
            pl.BlockSpec((half, D), const2, pipeline_mode=pl.Buffered(1)),
            pl.BlockSpec((half, D), const2, pipeline_mode=pl.Buffered(1)),
            pl.BlockSpec((1, D), const2),
            pl.BlockSpec((D, dff), const2, pipeline_mode=pl.Buffered(1)),
            pl.BlockSpec((dff, D), const2, pipeline_mode=pl.Buffered(1)),
            pl.BlockSpec((1, D), const2),
        ],
        out_specs=pl.BlockSpec((1, R, D), row_idx),
        out_shape=jax.ShapeDtypeStruct((B, T, D), F32),
        compiler_params=pltpu.CompilerParams(dimension_semantics=("arbitrary", "arbitrary"),
                                             vmem_limit_bytes=VMEM_LIMIT),
        name="out_mlp",
    )(x, o_mla, o_hg, mod4, woa, wob, nmlp, w1, w2, fnorm)


def _rope_slot(w):
    z = jnp.zeros(w.shape[:-1] + (MLA_NOPE,), w.dtype)
    z2 = jnp.zeros(w.shape[:-1] + (HEAD_SLOT - MLA_NOPE - MLA_ROPE,), w.dtype)
    return jnp.concatenate([z, w, z2], axis=-1)


def _pack_w_in(w_in):
    o = 0
    parts = []
    for n in (MLA_Q_RANK, MLA_KV_RANK, MLA_ROPE, HG_KW, HG_KW, HG_KW, HG_KW, HG_KW):
        parts.append(w_in[:, o:o + n])
        o += n
    cq, ckv, kr, hq, hf, hb, hi, hg = parts
    return jnp.concatenate([cq, ckv, _rope_slot(kr), hq, hf, hb, hi, hg], axis=1).astype(BF16)


def _pack_w_uq(w_uq):
    k = w_uq.shape[0]
    w3 = w_uq.reshape(k, MLA_HEADS, MLA_NOPE + MLA_ROPE)
    nope = w3[..., :MLA_NOPE].reshape(k, MLA_HEADS * MLA_NOPE)
    rope = w3[..., MLA_NOPE:].reshape(k, MLA_HEADS // 2, 2 * MLA_ROPE)
    rope = jnp.concatenate([jnp.zeros((k, MLA_HEADS // 2, LANES - 2 * MLA_ROPE), w_uq.dtype), rope], axis=-1)
    return jnp.concatenate([nope, rope.reshape(k, (MLA_HEADS // 2) * LANES)], axis=1).astype(BF16)


def _pack_w_ukv(w_ukv):
    k = w_ukv.shape[0]
    w3 = w_ukv.reshape(k, MLA_HEADS, MLA_NOPE + MLA_V)
    kk = w3[..., :MLA_NOPE].reshape(k, MLA_HEADS * MLA_NOPE)
    vv = w3[..., MLA_NOPE:].reshape(k, MLA_HEADS * MLA_V)
    return jnp.concatenate([kk, vv], axis=1).astype(BF16)


def _rope_tables(T):
    rows = T // GRID_W
    row = np.repeat(np.arange(rows), GRID_W).astype(np.float64)
    col = np.tile(np.arange(GRID_W), rows).astype(np.float64)
    n = MLA_ROPE // 4
    inv = ROPE_BASE ** (-np.arange(n, dtype=np.float64) / n)
    ar, ac = row[:, None] * inv, col[:, None] * inv
    cos32 = np.concatenate([np.cos(ar), np.cos(ar), np.cos(ac), np.cos(ac)], axis=-1)
    sin32 = np.concatenate([np.sin(ar), np.sin(ar), np.sin(ac), np.sin(ac)], axis=-1)
    cos_t = np.concatenate([np.ones((T, MLA_NOPE)), cos32, cos32], axis=-1)
    sin_t = np.concatenate([np.zeros((T, MLA_NOPE)), sin32, sin32], axis=-1)
    return jnp.asarray(cos_t, F32), jnp.asarray(sin_t, F32)


def _scan_constants(reverse):
    R = ROW_TILE
    t = np.arange(R, dtype=np.int32)[:, None]
    s = np.arange(R, dtype=np.int32)[None, :]
    if reverse:
        t, s = s, t
    tri = (t >= s)
    H = R // 2
    t, s = np.broadcast_to(t, (R, R))[:H, :H], np.broadcast_to(s, (R, R))[:H, :H]
    x = t ^ s
    lev = np.zeros((H, H), np.int32)
    for bit in range(1, N_LEVELS - 1):
        lev = np.where((x >> bit) > 0, bit, lev)
    lv = np.where(t > s, lev, np.where(t == s, N_LEVELS - 1, -1))
    return jnp.asarray(tri, BF16), jnp.asarray(lv, jnp.int32)


def kernel(x, c, ctx, c_ctx, w_ada, b_ada, norm_mix, w_in, q_norm, w_uq, kv_norm, w_ukv, hgrn_lb, hgrn_norm,
           w_out, norm_mlp, w_mlp_in, w_mlp_out, final_norm):
    B, T, D = x.shape
    L = ctx.shape[1]
    assert w_ada.shape[0] == 1, "single-layer block"
    assert T % ROW_TILE == 0 and L == ROW_TILE and T % GRID_W == 0
    assert T % MLP_ROW_TILE == 0 and T % ATTN_ROW_TILE == 0 and T % PROJ_ROW_TILE == 0
    assert PROJ_ROW_TILE % L == 0 and B % (PROJ_ROW_TILE // L) == 0 and T % L == 0 and B % HGRN_SAMPLES == 0

    pad_rows = (-(B + 1)) % 8
    cc = jnp.concatenate([c, c_ctx[None, :], jnp.zeros((pad_rows, D), F32)], axis=0)
    mod = _ada_call(cc, w_ada[0], b_ada[0][None, :])
    mod_lat = mod[:B].reshape(B, 6, D)
    mod_ctx = mod[B].reshape(1, 6, D)
    mod4 = mod_lat[:, 2:6]

    cos_t, sin_t = _rope_tables(T)
    win_p, wkv_p = _pack_w_in(w_in[0]), _pack_w_ukv(w_ukv[0])
    q, k, v, qh, lff, lfb, vi, g = _inproj_lat_call(
        x, mod_lat[:, 0:2], norm_mix, win_p, q_norm, _pack_w_uq(w_uq[0]), kv_norm, wkv_p, hgrn_lb, cos_t, sin_t)
    kc, vc, qhc, lffc, lfbc, vic = _inproj_ctx_call(
        ctx, mod_ctx[:, 0:2], norm_mix, win_p, kv_norm, wkv_p, hgrn_lb)

    tri_f, lv_f = _scan_constants(False)
    tri_b, lv_b = _scan_constants(True)
    o_f = _hgrn_call(False, (qh, lff, vi), (qhc, lffc, vic), tri_f, lv_f)
    o_hg = _hgrn_call(True, (qh, lfb, vi), (qhc, lfbc, vic), tri_b, lv_b, extra=(o_f, g, hgrn_norm))

    o_mla = _attn_call(q, k, kc, v, vc)

    half = MLA_HEADS * MLA_V
    wo = w_out[0].astype(BF16)
    return _mlp_call(x, o_mla, o_hg, mod4, wo[:half], wo[half:], norm_mlp, w_mlp_in[0].astype(BF16),
                     w_mlp_out[0].astype(BF16), final_norm[None, :])
```

```python
import functools

import jax
import jax.numpy as jnp
import numpy as np
from jax import lax
from jax.experimental import pallas as pl
from jax.experimental.pallas import tpu as pltpu

F32 = jnp.float32
BF16 = jnp.bfloat16

GRID_W = 64
MLA_HEADS = 8
MLA_Q_RANK = 256
MLA_KV_RANK = 128
MLA_NOPE = 64
MLA_ROPE = 32
MLA_V = 64
HG_HEADS = 4
HG_DK = 128
HG_DV = 128
HG_KW = HG_HEADS * HG_DK
ROPE_BASE = 10000.0
EPS = 1e-6
LOG2E = 1.4426950408889634

LANES = 128
SUBLANES = 8
HEAD_SLOT = LANES
ROW_TILE = 256
HGRN_SAMPLES = 2
MLP_ROW_TILE = 512
PROJ_ROW_TILE = 512
ATTN_ROW_TILE = 512
VMEM_LIMIT = 56 * 1024 * 1024

_C_CQ = (0, 256)
_C_CKV = (256, 384)
_C_KR = (384, 512)
_C_HQ = (512, 1024)
_C_HF = (1024, 1536)
_C_HB = (1536, 2048)
_C_HI = (2048, 2560)
_C_HG = (2560, 3072)
IN_P_WIDTH = 3072


def _rms(x, g):
    ms = jnp.mean(x * x, axis=-1, keepdims=True)
    return (x * lax.rsqrt(ms + EPS)) * g


def _silu(x):
    return x * jax.nn.sigmoid(x)


def _dot(a, b):
    return jnp.dot(a, b, preferred_element_type=F32)


def _dot_nt(a, b):
    return lax.dot_general(a, b, (((1,), (1,)), ((), ())), preferred_element_type=F32)


def _dot_tn(a, b):
    return lax.dot_general(a, b, (((0,), (0,)), ((), ())), preferred_element_type=F32)


def _ada_kernel(c_ref, w_ref, b_ref, o_ref):
    s = _silu(c_ref[...])
    o_ref[...] = jnp.dot(s, w_ref[...], preferred_element_type=F32,
                         precision=lax.Precision.HIGHEST) + b_ref[...]


def _ada_call(cc, w_ada, b_ada):
    rows, d = cc.shape
    n = w_ada.shape[1]
    tn = 1024
    return pl.pallas_call(
        _ada_kernel,
        grid=(n // tn,),
        in_specs=[
            pl.BlockSpec((rows, d), lambda j: (0, 0)),
            pl.BlockSpec((d, tn), lambda j: (0, j)),
            pl.BlockSpec((1, tn), lambda j: (0, j)),
        ],
        out_specs=pl.BlockSpec((rows, tn), lambda j: (0, j)),
        out_shape=jax.ShapeDtypeStruct((rows, n), F32),
        compiler_params=pltpu.CompilerParams(dimension_semantics=("arbitrary",), vmem_limit_bytes=VMEM_LIMIT),
        name="ada_mod",
    )(cc, w_ada, b_ada)


def _lane_lt(n_rows, bound):
    return lax.broadcasted_iota(jnp.int32, (n_rows, LANES), 1) < bound


def _rotate_half_lanes(x):
    lane = lax.broadcasted_iota(jnp.int32, x.shape, 1)
    return jnp.where((lane & 15) < 8, -pltpu.roll(x, LANES - 8, 1), pltpu.roll(x, 8, 1))


def _head_slots(nope, rope_lo, rope_hi):
    first = _lane_lt(nope.shape[0], MLA_NOPE)
    return (jnp.where(first, nope, rope_lo), jnp.where(first, pltpu.roll(nope, MLA_NOPE, 1), rope_hi))


def _inproj_kernel(is_ctx, *refs):
    if is_ctx:
        (x_ref, mod_ref, nmix_ref, win_ref, kvn_ref, wkv_ref, lb_ref,
         k_out, v_out, qh_out, lff_out, lfb_out, vi_out) = refs
    else:
        (x_ref, mod_ref, nmix_ref, win_ref, qn_ref, wq_ref, kvn_ref, wkv_ref, lb_ref, cos_ref, sin_ref,
         q_out, k_out, v_out, qh_out, lff_out, lfb_out, vi_out, g_out) = refs
    mod = mod_ref[0]
    h = _rms(x_ref[0], nmix_ref[...]) * (1.0 + mod[1:2]) + mod[0:1]
    hb = h.astype(BF16)
    n_pairs = MLA_HEADS // 2
    half = n_pairs * LANES

    def proj(c):
        return _dot(hb, win_ref[:, c[0]:c[1]])

    def put(ref, val):
        ref[...] = val.astype(ref.dtype).reshape(ref.shape)

    def pair(a, j):
        return a[:, j * LANES:(j + 1) * LANES]

    lbr = lb_ref[...]
    e = jnp.exp(lbr - jnp.max(lbr, axis=0, keepdims=True))
    lb = e[0] / jnp.sum(e, axis=0)
    lbf = lb[0:1]
    lbb = lb[1:2]

    ckv = proj(_C_CKV)
    kr = proj(_C_KR)
    if not is_ctx:
        cq = proj(_C_CQ)
    put(lff_out, jnp.log(lbf + (1.0 - lbf) * jax.nn.sigmoid(proj(_C_HF))))
    put(vi_out, proj(_C_HI))
    kv = _dot(_rms(ckv, kvn_ref[...]).astype(BF16), wkv_ref[...])
    put(lfb_out, jnp.log(lbb + (1.0 - lbb) * jax.nn.sigmoid(proj(_C_HB))))
    if not is_ctx:
        cos = cos_ref[...]
        sin = sin_ref[...]
        kr = kr * cos + _rotate_half_lanes(kr) * sin
        qq = _dot(_rms(cq, qn_ref[...]).astype(BF16), wq_ref[...])
    put(qh_out, _silu(proj(_C_HQ)))
    k_slots = []
    for j in range(n_pairs):
        k_slots += _head_slots(pair(kv, j), kr, kr)
    put(k_out, jnp.concatenate(k_slots, axis=1))
    lane = lax.broadcasted_iota(jnp.int32, (kv.shape[0], LANES), 1)
    ones_lane = jnp.where(lane == MLA_V, 1.0, 0.0)
    v_slots = []
    for j in range(n_pairs):
        v_slots += _head_slots(pair(kv, n_pairs + j), ones_lane, ones_lane)
    put(v_out, jnp.concatenate(v_slots, axis=1))
    if not is_ctx:
        scale = (MLA_NOPE + MLA_ROPE) ** -0.5 * LOG2E
        q_slots = []
        for j in range(n_pairs):
            r = pair(qq, n_pairs + j)
            r = (r * cos + _rotate_half_lanes(r) * sin) * scale
            q_slots += _head_slots(pair(qq, j) * scale, r, pltpu.roll(r, LANES - MLA_ROPE, 1))
        put(q_out, jnp.concatenate(q_slots, axis=1))
        put(g_out, _silu(proj(_C_HG)))


def _inproj_out_shapes(B, rows, latent):
    nq = MLA_HEADS * HEAD_SLOT
    shapes = [
        ((B, rows, nq), BF16),
        ((B, rows, nq), BF16),
        ((B, rows, HG_KW), BF16),
        ((B, rows, HG_KW), F32),
        ((B, rows, HG_KW), F32),
        ((B, rows, HG_KW), BF16),
    ]
    if latent:
        shapes = [((B, rows, nq), BF16)] + shapes + [((B, rows, HG_KW), BF16)]
    return shapes


def _inproj_lat_call(x, mod_lat, nmix, win_p, qn, wq_p, kvn, wkv_p, lb_raw, cos_t, sin_t):
    B, T, D = x.shape
    R = PROJ_ROW_TILE
    nq = MLA_HEADS * HEAD_SLOT
    row_idx = lambda b, i: (b, i, 0)
    const2 = lambda b, i: (0, 0)
    in_specs = [
        pl.BlockSpec((1, R, D), row_idx),
        pl.BlockSpec((1, 2, D), lambda b, i: (b, 0, 0)),
        pl.BlockSpec((1, D), const2),
        pl.BlockSpec((D, IN_P_WIDTH), const2, pipeline_mode=pl.Buffered(1)),
        pl.BlockSpec((1, MLA_Q_RANK), const2),
        pl.BlockSpec((MLA_Q_RANK, nq), const2, pipeline_mode=pl.Buffered(1)),
        pl.BlockSpec((1, MLA_KV_RANK), const2),
        pl.BlockSpec((MLA_KV_RANK, nq), const2, pipeline_mode=pl.Buffered(1)),
        pl.BlockSpec(lb_raw.shape, lambda b, i: (0, 0, 0)),
        pl.BlockSpec((R, LANES), lambda b, i: (i, 0)),
        pl.BlockSpec((R, LANES), lambda b, i: (i, 0)),
    ]
    shapes = _inproj_out_shapes(B, T, True)
    return pl.pallas_call(
        functools.partial(_inproj_kernel, False),
        grid=(B, T // R),
        in_specs=in_specs,
        out_specs=[pl.BlockSpec((1, R, s[2]), row_idx) for s, _ in shapes],
        out_shape=[jax.ShapeDtypeStruct(s, d) for s, d in shapes],
        compiler_params=pltpu.CompilerParams(dimension_semantics=("arbitrary", "arbitrary"),
                                             vmem_limit_bytes=VMEM_LIMIT),
        name="inproj_lat",
    )(x, mod_lat, nmix, win_p, qn, wq_p, kvn, wkv_p, lb_raw, cos_t, sin_t)


def _inproj_ctx_call(ctx, mod_ctx, nmix, win_p, kvn, wkv_p, lb_raw):
    B, L, D = ctx.shape
    R = PROJ_ROW_TILE
    per = R // L
    nq = MLA_HEADS * HEAD_SLOT
    x2 = ctx.reshape(B // per, R, D)
    const2 = lambda i: (0, 0)
    in_specs = [
        pl.BlockSpec((1, R, D), lambda i: (i, 0, 0)),
        pl.BlockSpec((1, 2, D), lambda i: (0, 0, 0)),
        pl.BlockSpec((1, D), const2),
        pl.BlockSpec((D, IN_P_WIDTH), const2, pipeline_mode=pl.Buffered(1)),
        pl.BlockSpec((1, MLA_KV_RANK), const2),
        pl.BlockSpec((MLA_KV_RANK, nq), const2, pipeline_mode=pl.Buffered(1)),
        pl.BlockSpec(lb_raw.shape, lambda i: (0, 0, 0)),
    ]
    shapes = _inproj_out_shapes(B, L, False)
    return pl.pallas_call(
        functools.partial(_inproj_kernel, True),
        grid=(B // per,),
        in_specs=in_specs,
        out_specs=[pl.BlockSpec((per, L, s[2]), lambda i: (i, 0, 0)) for s, _ in shapes],
        out_shape=[jax.ShapeDtypeStruct(s, d) for s, d in shapes],
        compiler_params=pltpu.CompilerParams(dimension_semantics=("arbitrary",), vmem_limit_bytes=VMEM_LIMIT),
        name="inproj_ctx",
    )(x2, mod_ctx, nmix, win_p, kvn, wkv_p, lb_raw)


N_LEVELS = 8
N_SMALL_LEVELS = 2


def _neg_abs(x):
    bits = lax.bitcast_convert_type(x, jnp.uint32) | jnp.uint32(0x80000000)
    return lax.bitcast_convert_type(bits, F32)


def _replace_bit(y, row, bit, value):
    step = 1 << bit
    n = y.shape[0]
    has = ((row >> bit) & 1) == 1
    if value == 1:
        return jnp.where(has, y, pltpu.roll(y, n - step, 0))
    return jnp.where(has, pltpu.roll(y, step, 0), y)


def _hgrn_kernel(reverse, *refs):
    st_ref = refs[-1]
    i = pl.program_id(1)

    @pl.when(i == 0)
    def _():
        st_ref[...] = jnp.zeros_like(st_ref)

    stages = [_hgrn_sample(reverse, n, i == 0, *refs) for n in range(refs[0].shape[0])]
    while stages:
        stages = [s for s in stages if next(s, False)]


def _hgrn_sample(reverse, n, is_ctx, qh_ref, lf_ref, vi_ref, qhc_ref, lfc_ref, vic_ref, tri_ref, lv_ref, *rest):
    if reverse:
        of_ref, g_ref, hgn_ref, out_ref, st_ref = rest
    else:
        out_ref, st_ref = rest
    lf = jnp.where(is_ctx, lfc_ref[n], lf_ref[n])
    qh_in = jnp.where(is_ctx, qhc_ref[n], qh_ref[n])
    vi_in = jnp.where(is_ctx, vic_ref[n], vi_ref[n])
    l_hi = lf.astype(BF16)
    r1 = lf - l_hi.astype(F32)
    l_mid = r1.astype(BF16)
    l_lo = (r1 - l_mid.astype(F32)).astype(BF16)
    tri = tri_ref[...]
    b_all = (_dot(tri, l_hi) + _dot(tri, l_mid) + _dot(tri, l_lo)) * LOG2E
    f_all = jnp.exp(lf)
    k_all = 1.0 - f_all
    q_all = qh_in.astype(F32)
    lv = lv_ref[...]
    R = lf.shape[0]
    H = R // 2
    row = lax.broadcasted_iota(jnp.int32, (R, LANES), 0)
    near, far = (0, 1) if not reverse else (1, 0)
    end_row = 0 if reverse else R - 1
    heads = range(HG_HEADS)
    sls = [slice(h * HG_DK, (h + 1) * HG_DK) for h in heads]
    bs = [b_all[:, sl] for sl in sls]
    qs = [q_all[:, sl] for sl in sls]
    ks = [k_all[:, sl] for sl in sls]
    fs = [f_all[:, sl] for sl in sls]
    vs = [vi_in[:, sl] for sl in sls]

    diag = lv == N_LEVELS - 1
    a0 = [jnp.where(diag, _dot_nt(qs[h][:H].astype(BF16), ks[h][:H].astype(BF16)), 0.0) for h in heads]
    a1 = [jnp.where(diag, _dot_nt(qs[h][H:].astype(BF16), ks[h][H:].astype(BF16)), 0.0) for h in heads]
    a_off = [None] * HG_HEADS
    zs = list(bs)
    for lev in range(N_LEVELS):
        m = 1 << lev
        mask = lv == lev
        q_side = ((row >> lev) & 1) == far
        for h in heads:
            bh, qh, kh = bs[h], qs[h], ks[h]
            if lev == 0:
                zs[h] = _replace_bit(zs[h], row, lev, far)
                u = jnp.where(q_side, qh * fs[h], kh).astype(BF16)
            elif lev < N_SMALL_LEVELS:
                bmid = _replace_bit(zs[h], row, lev, near)
                zs[h] = _replace_bit(zs[h], row, lev, far)
                qk = jnp.where(q_side, qh, kh)
                arg = _neg_abs(bh - bmid)
            elif 2 * m == SUBLANES:
                ref = m if reverse else m - 1
                bmid = jnp.concatenate([jnp.broadcast_to(bh[lo + ref:lo + ref + 1], (SUBLANES, HG_DK))
                                        for lo in range(0, R, SUBLANES)], axis=0)
                qk = jnp.where(q_side, qh, kh)
                arg = _neg_abs(bh - bmid)
            else:
                args, parts = [], []
                for j in range(R // (2 * m)):
                    lo = j * 2 * m
                    ref = lo + (m if reverse else m - 1)
                    bref = bh[ref:ref + 1]
                    if reverse:
                        args += [bh[lo:lo + m] - bref, bref - bh[lo + m:lo + 2 * m]]
                        parts += [qh[lo:lo + m], kh[lo + m:lo + 2 * m]]
                    else:
                        args += [bref - bh[lo:lo + m], bh[lo + m:lo + 2 * m] - bref]
                        parts += [kh[lo:lo + m], qh[lo + m:lo + 2 * m]]
                arg = jnp.concatenate(args, axis=0)
                qk = jnp.concatenate(parts, axis=0)
            if lev > 0:
                u = (qk * jnp.exp2(arg)).astype(BF16)
            if lev < N_LEVELS - 1:
                a0[h] = jnp.where(mask, _dot_nt(u[:H], u[:H]), a0[h])
                a1[h] = jnp.where(mask, _dot_nt(u[H:], u[H:]), a1[h])
            elif reverse:
                a_off[h] = _dot_nt(u[:H], u[H:])
            else:
                a_off[h] = _dot_nt(u[H:], u[:H])
        yield True

    for h in heads:
        sl, bh, qh, kh, vh = sls[h], bs[h], qs[h], ks[h], vs[h]
        zend = bh[end_row:end_row + 1]
        st = st_ref[n * HG_HEADS + h]
        if reverse:
            o_in = jnp.concatenate([
                _dot(jnp.concatenate([a0[h], a_off[h]], axis=1).astype(BF16), vh),
                _dot(a1[h].astype(BF16), vh[H:])], axis=0)
        else:
            o_in = jnp.concatenate([
                _dot(a0[h].astype(BF16), vh[:H]),
                _dot(jnp.concatenate([a_off[h], a1[h]], axis=1).astype(BF16), vh)], axis=0)
        o = o_in + _dot_nt((qh * jnp.exp2(bh)).astype(BF16), st.astype(BF16))
        k_end = (kh * jnp.exp2(zend - bh)).astype(BF16)
        st_ref[n * HG_HEADS + h] = jnp.exp2(zend) * st + _dot_tn(vh, k_end)

        if reverse:
            tot = of_ref[n, :, sl] + o
            out_ref[n, :, sl] = (_rms(tot, hgn_ref[...]) * g_ref[n, :, sl].astype(F32)).astype(out_ref.dtype)
        else:
            out_ref[n, :, sl] = o
        yield True


def _hgrn_call(reverse, lat, ctx, tri, lv, extra=()):
    B, T, KW = lat[0].shape
    R = ROW_TILE
    n_lat = T // R

    if reverse:
        def lat_idx(b, i):
            return (b, jnp.where(i == 0, n_lat - 1, n_lat - i), 0)
    else:
        def lat_idx(b, i):
            return (b, jnp.maximum(i - 1, 0), 0)

    NS = HGRN_SAMPLES
    const2 = lambda b, i: (0, 0)
    in_specs = [pl.BlockSpec((NS, R, KW), lat_idx)] * 3 + [pl.BlockSpec((NS, R, KW), lambda b, i: (b, 0, 0))] * 3 + [
        pl.BlockSpec((R, R), const2),
        pl.BlockSpec((R // 2, R // 2), const2),
    ]
    if reverse:
        in_specs += [
            pl.BlockSpec((NS, R, KW), lat_idx),
            pl.BlockSpec((NS, R, KW), lat_idx),
            pl.BlockSpec((1, HG_DV), const2),
        ]
        out_dtype = BF16
    else:
        out_dtype = F32
    return pl.pallas_call(
        functools.partial(_hgrn_kernel, reverse),
        grid=(B // NS, n_lat + 1),
        in_specs=in_specs,
        out_specs=pl.BlockSpec((NS, R, KW), lat_idx),
        out_shape=jax.ShapeDtypeStruct((B, T, KW), out_dtype),
        scratch_shapes=[pltpu.VMEM((NS * HG_HEADS, HG_DV, HG_DK), F32)],
        compiler_params=pltpu.CompilerParams(dimension_semantics=("arbitrary", "arbitrary"),
                                             vmem_limit_bytes=VMEM_LIMIT),
        name="hgrn_bwd" if reverse else "hgrn_fwd",
    )(*lat, *ctx, tri, lv, *extra)


def _attn_kernel(q_ref, k_ref, kc_ref, v_ref, vc_ref, o_ref):
    n_lat = k_ref.shape[1]

    def scores(h):
        sl = slice(h * HEAD_SLOT, (h + 1) * HEAD_SLOT)
        qh = q_ref[0, :, sl]
        return jnp.concatenate([_dot_nt(qh, k_ref[0, :, sl]), _dot_nt(qh, kc_ref[0, :, sl])], axis=1)

    def attend(h, s):
        sl = slice(h * HEAD_SLOT, (h + 1) * HEAD_SLOT)
        pb = jnp.exp2(s - jnp.max(s, axis=-1, keepdims=True)).astype(BF16)
        o = _dot(pb[:, :n_lat], v_ref[0, :, sl]) + _dot(pb[:, n_lat:], vc_ref[0, :, sl])
        return o * (1.0 / o[:, MLA_V:MLA_V + 1])

    outs = []
    s_next = scores(0)
    for h in range(MLA_HEADS):
        s = s_next
        if h + 1 < MLA_HEADS:
            s_next = scores(h + 1)
        outs.append(attend(h, s))
    first = _lane_lt(q_ref.shape[1], MLA_V)
    pairs = [jnp.where(first, outs[2 * j], pltpu.roll(outs[2 * j + 1], MLA_V, 1)) for j in range(MLA_HEADS // 2)]
    o_ref[0] = jnp.concatenate(pairs, axis=-1).astype(o_ref.dtype)


def _attn_call(q, k, k_ctx, v, v_ctx):
    B, T, nq = q.shape
    R = ATTN_ROW_TILE
    whole = lambda a: pl.BlockSpec((1,) + a.shape[1:], lambda b, i: (b, 0, 0))
    return pl.pallas_call(
        _attn_kernel,
        grid=(B, T // R),
        in_specs=[pl.BlockSpec((1, R, nq), lambda b, i: (b, i, 0)), whole(k), whole(k_ctx), whole(v), whole(v_ctx)],
        out_specs=pl.BlockSpec((1, R, MLA_HEADS * MLA_V), lambda b, i: (b, i, 0)),
        out_shape=jax.ShapeDtypeStruct((B, T, MLA_HEADS * MLA_V), BF16),
        compiler_params=pltpu.CompilerParams(dimension_semantics=("arbitrary", "arbitrary"),
                                             vmem_limit_bytes=VMEM_LIMIT),
        name="mla_attn",
    )(q, k, k_ctx, v, v_ctx)


def _mlp_kernel(n_ff_chunks, x_ref, om_ref, oh_ref, mod_ref, woa_ref, wob_ref, nmlp_ref, w1_ref, w2_ref,
                fn_ref, o_ref):
    mod = mod_ref[0]
    y = _dot(om_ref[0], woa_ref[...]) + _dot(oh_ref[0], wob_ref[...])
    x1 = x_ref[0] + mod[0:1] * y
    h2 = (_rms(x1, nmlp_ref[...]) * (1.0 + mod[2:3]) + mod[1:2]).astype(BF16)
    ff = w1_ref.shape[1] // n_ff_chunks
    acc = jnp.zeros(x1.shape, F32)
    for c in range(n_ff_chunks):
        u = jnp.maximum(_dot(h2, w1_ref[:, c * ff:(c + 1) * ff]), 0.0)
        acc = acc + _dot((u * u).astype(BF16), w2_ref[c * ff:(c + 1) * ff, :])
    x2 = x1 + mod[3:4] * acc
    o_ref[0] = _rms(x2, fn_ref[...])


def _mlp_call(x, o_mla, o_hg, mod4, woa, wob, nmlp, w1, w2, fnorm):
    B, T, D = x.shape
    R = MLP_ROW_TILE
    dff = w1.shape[1]
    half = o_mla.shape[2]
    row_idx = lambda b, i: (b, i, 0)
    const2 = lambda b, i: (0, 0)
    return pl.pallas_call(
        functools.partial(_mlp_kernel, 4),
        grid=(B, T // R),
        in_specs=[
            pl.BlockSpec((1, R, D), row_idx),
            pl.BlockSpec((1, R, half), row_idx),
            pl.BlockSpec((1, R, half), row_idx),
            pl.BlockSpec((1, 4, D), lambda b, i: (b, 0, 0)),
            pl.BlockSpec((half, D), const2, pipeline_mode=pl.Buffered(1)),
            pl.BlockSpec((half, D), const2, pipeline_mode=pl.Buffered(1)),
            pl.BlockSpec((1, D), const2),
            pl.BlockSpec((D, dff), const2, pipeline_mode=pl.Buffered(1)),
            pl.BlockSpec((dff, D), const2, pipeline_mode=pl.Buffered(1)),
            pl.BlockSpec((1, D), const2),
        ],
        out_specs=pl.BlockSpec((1, R, D), row_idx),
        out_shape=jax.ShapeDtypeStruct((B, T, D), F32),
        compiler_params=pltpu.CompilerParams(dimension_semantics=("arbitrary", "arbitrary"),
                                             vmem_limit_bytes=VMEM_LIMIT),
        name="out_mlp",
    )(x, o_mla, o_hg, mod4, woa, wob, nmlp, w1, w2, fnorm)


def _rope_slot(w):
    z = jnp.zeros(w.shape[:-1] + (MLA_NOPE,), w.dtype)
    z2 = jnp.zeros(w.shape[:-1] + (HEAD_SLOT - MLA_NOPE - MLA_ROPE,), w.dtype)
    return jnp.concatenate([z, w, z2], axis=-1)


def _pack_w_in(w_in):
    o = 0
    parts = []
    for n in (MLA_Q_RANK, MLA_KV_RANK, MLA_ROPE, HG_KW, HG_KW, HG_KW, HG_KW, HG_KW):
        parts.append(w_in[:, o:o + n])
        o += n
    cq, ckv, kr, hq, hf, hb, hi, hg = parts
    return jnp.concatenate([cq, ckv, _rope_slot(kr), hq, hf, hb, hi, hg], axis=1).astype(BF16)


def _pack_w_uq(w_uq):
    k = w_uq.shape[0]
    w3 = w_uq.reshape(k, MLA_HEADS, MLA_NOPE + MLA_ROPE)
    nope = w3[..., :MLA_NOPE].reshape(k, MLA_HEADS * MLA_NOPE)
    rope = w3[..., MLA_NOPE:].reshape(k, MLA_HEADS // 2, 2 * MLA_ROPE)
    rope = jnp.concatenate([jnp.zeros((k, MLA_HEADS // 2, LANES - 2 * MLA_ROPE), w_uq.dtype), rope], axis=-1)
    return jnp.concatenate([nope, rope.reshape(k, (MLA_HEADS // 2) * LANES)], axis=1).astype(BF16)


def _pack_w_ukv(w_ukv):
    k = w_ukv.shape[0]
    w3 = w_ukv.reshape(k, MLA_HEADS, MLA_NOPE + MLA_V)
    kk = w3[..., :MLA_NOPE].reshape(k, MLA_HEADS * MLA_NOPE)
    vv = w3[..., MLA_NOPE:].reshape(k, MLA_HEADS * MLA_V)
    return jnp.concatenate([kk, vv], axis=1).astype(BF16)


def _rope_tables(T):
    rows = T // GRID_W
    row = np.repeat(np.arange(rows), GRID_W).astype(np.float64)
    col = np.tile(np.arange(GRID_W), rows).astype(np.float64)
    n = MLA_ROPE // 4
    inv = ROPE_BASE ** (-np.arange(n, dtype=np.float64) / n)
    ar, ac = row[:, None] * inv, col[:, None] * inv
    cos32 = np.concatenate([np.cos(ar), np.cos(ar), np.cos(ac), np.cos(ac)], axis=-1)
    sin32 = np.concatenate([np.sin(ar), np.sin(ar), np.sin(ac), np.sin(ac)], axis=-1)
    cos_t = np.concatenate([np.ones((T, MLA_NOPE)), cos32, cos32], axis=-1)
    sin_t = np.concatenate([np.zeros((T, MLA_NOPE)), sin32, sin32], axis=-1)
    return jnp.asarray(cos_t, F32), jnp.asarray(sin_t, F32)


def _scan_constants(reverse):
    R = ROW_TILE
    t = np.arange(R, dtype=np.int32)[:, None]
    s = np.arange(R, dtype=np.int32)[None, :]
    if reverse:
        t, s = s, t
    tri = (t >= s)
    H = R // 2
    t, s = np.broadcast_to(t, (R, R))[:H, :H], np.broadcast_to(s, (R, R))[:H, :H]
    x = t ^ s
    lev = np.zeros((H, H), np.int32)
    for bit in range(1, N_LEVELS - 1):
        lev = np.where((x >> bit) > 0, bit, lev)
    lv = np.where(t > s, lev, np.where(t == s, N_LEVELS - 1, -1))
    return jnp.asarray(tri, BF16), jnp.asarray(lv, jnp.int32)


def kernel(x, c, ctx, c_ctx, w_ada, b_ada, norm_mix, w_in, q_norm, w_uq, kv_norm, w_ukv, hgrn_lb, hgrn_norm,
           w_out, norm_mlp, w_mlp_in, w_mlp_out, final_norm):
    B, T, D = x.shape
    L = ctx.shape[1]
    assert w_ada.shape[0] == 1, "single-layer block"
    assert T % ROW_TILE == 0 and L == ROW_TILE and T % GRID_W == 0
    assert T % MLP_ROW_TILE == 0 and T % ATTN_ROW_TILE == 0 and T % PROJ_ROW_TILE == 0
    assert PROJ_ROW_TILE % L == 0 and B % (PROJ_ROW_TILE // L) == 0 and T % L == 0 and B % HGRN_SAMPLES == 0

    pad_rows = (-(B + 1)) % 8
    cc = jnp.concatenate([c, c_ctx[None, :], jnp.zeros((pad_rows, D), F32)], axis=0)
    mod = _ada_call(cc, w_ada[0], b_ada[0][None, :])
    mod_lat = mod[:B].reshape(B, 6, D)
    mod_ctx = mod[B].reshape(1, 6, D)
    mod4 = mod_lat[:, 2:6]

    cos_t, sin_t = _rope_tables(T)
    win_p, wkv_p = _pack_w_in(w_in[0]), _pack_w_ukv(w_ukv[0])
    q, k, v, qh, lff, lfb, vi, g = _inproj_lat_call(
        x, mod_lat[:, 0:2], norm_mix, win_p, q_norm, _pack_w_uq(w_uq[0]), kv_norm, wkv_p, hgrn_lb, cos_t, sin_t)
    kc, vc, qhc, lffc, lfbc, vic = _inproj_ctx_call(
        ctx, mod_ctx[:, 0:2], norm_mix, win_p, kv_norm, wkv_p, hgrn_lb)

    tri_f, lv_f = _scan_constants(False)
    tri_b, lv_b = _scan_constants(True)
    o_f = _hgrn_call(False, (qh, lff, vi), (qhc, lffc, vic), tri_f, lv_f)
    o_hg = _hgrn_call(True, (qh, lfb, vi), (qhc, lfbc, vic), tri_b, lv_b, extra=(o_f, g, hgrn_norm))

    o_mla = _attn_call(q, k, kc, v, vc)

    half = MLA_HEADS * MLA_V
    wo = w_out[0].astype(BF16)
    return _mlp_call(x, o_mla, o_hg, mod4, wo[:half], wo[half:], norm_mlp, w_mlp_in[0].astype(BF16),
                     w_mlp_out[0].astype(BF16), final_norm[None, :])
```

```python
import functools

import jax
import jax.numpy as jnp
import numpy as np
from jax import lax
from jax.experimental import pallas as pl
from jax.experimental.pallas import tpu as pltpu

F32 = jnp.float32
BF16 = jnp.bfloat16

GRID_W = 64
MLA_HEADS = 8
MLA_Q_RANK = 256
MLA_KV_RANK = 128
MLA_NOPE = 64
MLA_ROPE = 32
MLA_V = 64
HG_HEADS = 4
HG_DK = 128
HG_DV = 128
HG_KW = HG_HEADS * HG_DK
ROPE_BASE = 10000.0
EPS = 1e-6
LOG2E = 1.4426950408889634

LANES = 128
SUBLANES = 8
HEAD_SLOT = LANES
ROW_TILE = 256
HGRN_SAMPLES = 4
MLP_ROW_TILE = 512
PROJ_ROW_TILE = 512
ATTN_ROW_TILE = 512
VMEM_LIMIT = 56 * 1024 * 1024

_C_CQ = (0, 256)
_C_CKV = (256, 384)
_C_KR = (384, 512)
_C_HQ = (512, 1024)
_C_HF = (1024, 1536)
_C_HB = (1536, 2048)
_C_HI = (2048, 2560)
_C_HG = (2560, 3072)
IN_P_WIDTH = 3072


def _rms(x, g):
    ms = jnp.mean(x * x, axis=-1, keepdims=True)
    return (x * lax.rsqrt(ms + EPS)) * g


def _silu(x):
    return x * jax.nn.sigmoid(x)


def _dot(a, b):
    return jnp.dot(a, b, preferred_element_type=F32)


def _dot_nt(a, b):
    return lax.dot_general(a, b, (((1,), (1,)), ((), ())), preferred_element_type=F32)


def _dot_tn(a, b):
    return lax.dot_general(a, b, (((0,), (0,)), ((), ())), preferred_element_type=F32)


def _ada_kernel(c_ref, w_ref, b_ref, o_ref):
    s = _silu(c_ref[...])
    o_ref[...] = jnp.dot(s, w_ref[...], preferred_element_type=F32,
                         precision=lax.Precision.HIGHEST) + b_ref[...]


def _ada_call(cc, w_ada, b_ada):
    rows, d = cc.shape
    n = w_ada.shape[1]
    tn = 1024
    return pl.pallas_call(
        _ada_kernel,
        grid=(n // tn,),
        in_specs=[
            pl.BlockSpec((rows, d), lambda j: (0, 0)),
            pl.BlockSpec((d, tn), lambda j: (0, j)),
            pl.BlockSpec((1, tn), lambda j: (0, j)),
        ],
        out_specs=pl.BlockSpec((rows, tn), lambda j: (0, j)),
        out_shape=jax.ShapeDtypeStruct((rows, n), F32),
        compiler_params=pltpu.CompilerParams(dimension_semantics=("arbitrary",), vmem_limit_bytes=VMEM_LIMIT),
        name="ada_mod",
    )(cc, w_ada, b_ada)


def _lane_lt(n_rows, bound):
    return lax.broadcasted_iota(jnp.int32, (n_rows, LANES), 1) < bound


def _rotate_half_lanes(x):
    lane = lax.broadcasted_iota(jnp.int32, x.shape, 1)
    return jnp.where((lane & 15) < 8, -pltpu.roll(x, LANES - 8, 1), pltpu.roll(x, 8, 1))


def _head_slots(nope, rope_lo, rope_hi):
    first = _lane_lt(nope.shape[0], MLA_NOPE)
    return (jnp.where(first, nope, rope_lo), jnp.where(first, pltpu.roll(nope, MLA_NOPE, 1), rope_hi))


def _inproj_kernel(is_ctx, *refs):
    if is_ctx:
        (x_ref, mod_ref, nmix_ref, win_ref, kvn_ref, wkv_ref, lb_ref,
         k_out, v_out, qh_out, lff_out, lfb_out, vi_out) = refs
    else:
        (x_ref, mod_ref, nmix_ref, win_ref, qn_ref, wq_ref, kvn_ref, wkv_ref, lb_ref, cos_ref, sin_ref,
         q_out, k_out, v_out, qh_out, lff_out, lfb_out, vi_out, g_out) = refs
    mod = mod_ref[0]
    h = _rms(x_ref[0], nmix_ref[...]) * (1.0 + mod[1:2]) + mod[0:1]
    hb = h.astype(BF16)
    n_pairs = MLA_HEADS // 2
    half = n_pairs * LANES

    def proj(c):
        return _dot(hb, win_ref[:, c[0]:c[1]])

    def put(ref, val):
        ref[...] = val.astype(ref.dtype).reshape(ref.shape)

    def pair(a, j):
        return a[:, j * LANES:(j + 1) * LANES]

    lbr = lb_ref[...]
    e = jnp.exp(lbr - jnp.max(lbr, axis=0, keepdims=True))
    lb = e[0] / jnp.sum(e, axis=0)
    lbf = lb[0:1]
    lbb = lb[1:2]

    ckv = proj(_C_CKV)
    kr = proj(_C_KR)
    if not is_ctx:
        cq = proj(_C_CQ)
    put(lff_out, jnp.log(lbf + (1.0 - lbf) * jax.nn.sigmoid(proj(_C_HF))))
    put(vi_out, proj(_C_HI))
    kv = _dot(_rms(ckv, kvn_ref[...]).astype(BF16), wkv_ref[...])
    put(lfb_out, jnp.log(lbb + (1.0 - lbb) * jax.nn.sigmoid(proj(_C_HB))))
    if not is_ctx:
        cos = cos_ref[...]
        sin = sin_ref[...]
        kr = kr * cos + _rotate_half_lanes(kr) * sin
        qq = _dot(_rms(cq, qn_ref[...]).astype(BF16), wq_ref[...])
    put(qh_out, _silu(proj(_C_HQ)))
    k_slots = []
    for j in range(n_pairs):
        k_slots += _head_slots(pair(kv, j), kr, kr)
    put(k_out, jnp.concatenate(k_slots, axis=1))
    lane = lax.broadcasted_iota(jnp.int32, (kv.shape[0], LANES), 1)
    ones_lane = jnp.where(lane == MLA_V, 1.0, 0.0)
    v_slots = []
    for j in range(n_pairs):
        v_slots += _head_slots(pair(kv, n_pairs + j), ones_lane, ones_lane)
    put(v_out, jnp.concatenate(v_slots, axis=1))
    if not is_ctx:
        scale = (MLA_NOPE + MLA_ROPE) ** -0.5 * LOG2E
        q_slots = []
        for j in range(n_pairs):
            r = pair(qq, n_pairs + j)
            r = (r * cos + _rotate_half_lanes(r) * sin) * scale
            q_slots += _head_slots(pair(qq, j) * scale, r, pltpu.roll(r, LANES - MLA_ROPE, 1))
        put(q_out, jnp.concatenate(q_slots, axis=1))
        put(g_out, _silu(proj(_C_HG)))


def _inproj_out_shapes(B, rows, latent):
    nq = MLA_HEADS * HEAD_SLOT
    shapes = [
        ((B, rows, nq), BF16),
        ((B, rows, nq), BF16),
        ((B, rows, HG_KW), BF16),
        ((B, rows, HG_KW), F32),
        ((B, rows, HG_KW), F32),
        ((B, rows, HG_KW), BF16),
    ]
    if latent:
        shapes = [((B, rows, nq), BF16)] + shapes + [((B, rows, HG_KW), BF16)]
    return shapes


def _inproj_lat_call(x, mod_lat, nmix, win_p, qn, wq_p, kvn, wkv_p, lb_raw, cos_t, sin_t):
    B, T, D = x.shape
    R = PROJ_ROW_TILE
    nq = MLA_HEADS * HEAD_SLOT
    row_idx = lambda b, i: (b, i, 0)
    const2 = lambda b, i: (0, 0)
    in_specs = [
        pl.BlockSpec((1, R, D), row_idx),
        pl.BlockSpec((1, 2, D), lambda b, i: (b, 0, 0)),
        pl.BlockSpec((1, D), const2),
        pl.BlockSpec((D, IN_P_WIDTH), const2, pipeline_mode=pl.Buffered(1)),
        pl.BlockSpec((1, MLA_Q_RANK), const2),
        pl.BlockSpec((MLA_Q_RANK, nq), const2, pipeline_mode=pl.Buffered(1)),
        pl.BlockSpec((1, MLA_KV_RANK), const2),
        pl.BlockSpec((MLA_KV_RANK, nq), const2, pipeline_mode=pl.Buffered(1)),
        pl.BlockSpec(lb_raw.shape, lambda b, i: (0, 0, 0)),
        pl.BlockSpec((R, LANES), lambda b, i: (i, 0)),
        pl.BlockSpec((R, LANES), lambda b, i: (i, 0)),
    ]
    shapes = _inproj_out_shapes(B, T, True)
    return pl.pallas_call(
        functools.partial(_inproj_kernel, False),
        grid=(B, T // R),
        in_specs=in_specs,
        out_specs=[pl.BlockSpec((1, R, s[2]), row_idx) for s, _ in shapes],
        out_shape=[jax.ShapeDtypeStruct(s, d) for s, d in shapes],
        compiler_params=pltpu.CompilerParams(dimension_semantics=("arbitrary", "arbitrary"),
                                             vmem_limit_bytes=VMEM_LIMIT),
        name="inproj_lat",
    )(x, mod_lat, nmix, win_p, qn, wq_p, kvn, wkv_p, lb_raw, cos_t, sin_t)


def _inproj_ctx_call(ctx, mod_ctx, nmix, win_p, kvn, wkv_p, lb_raw):
    B, L, D = ctx.shape
    R = PROJ_ROW_TILE
    per = R // L
    nq = MLA_HEADS * HEAD_SLOT
    x2 = ctx.reshape(B // per, R, D)
    const2 = lambda i: (0, 0)
    in_specs = [
        pl.BlockSpec((1, R, D), lambda i: (i, 0, 0)),
        pl.BlockSpec((1, 2, D), lambda i: (0, 0, 0)),
        pl.BlockSpec((1, D), const2),
        pl.BlockSpec((D, IN_P_WIDTH), const2, pipeline_mode=pl.Buffered(1)),
        pl.BlockSpec((1, MLA_KV_RANK), const2),
        pl.BlockSpec((MLA_KV_RANK, nq), const2, pipeline_mode=pl.Buffered(1)),
        pl.BlockSpec(lb_raw.shape, lambda i: (0, 0, 0)),
    ]
    shapes = _inproj_out_shapes(B, L, False)
    return pl.pallas_call(
        functools.partial(_inproj_kernel, True),
        grid=(B // per,),
        in_specs=in_specs,
        out_specs=[pl.BlockSpec((per, L, s[2]), lambda i: (i, 0, 0)) for s, _ in shapes],
        out_shape=[jax.ShapeDtypeStruct(s, d) for s, d in shapes],
        compiler_params=pltpu.CompilerParams(dimension_semantics=("arbitrary",), vmem_limit_bytes=VMEM_LIMIT),
        name="inproj_ctx",
    )(x2, mod_ctx, nmix, win_p, kvn, wkv_p, lb_raw)


N_LEVELS = 8
N_SMALL_LEVELS = 2


def _neg_abs(x):
    bits = lax.bitcast_convert_type(x, jnp.uint32) | jnp.uint32(0x80000000)
    return lax.bitcast_convert_type(bits, F32)


def _replace_bit(y, row, bit, value):
    step = 1 << bit
    n = y.shape[0]
    has = ((row >> bit) & 1) == 1
    if value == 1:
        return jnp.where(has, y, pltpu.roll(y, n - step, 0))
    return jnp.where(has, pltpu.roll(y, step, 0), y)


def _hgrn_kernel(reverse, *refs):
    st_ref = refs[-1]
    i = pl.program_id(1)

    @pl.when(i == 0)
    def _():
        st_ref[...] = jnp.zeros_like(st_ref)

    stages = [_hgrn_sample(reverse, n, i == 0, *refs) for n in range(refs[0].shape[0])]
    while stages:
        stages = [s for s in stages if next(s, False)]


def _hgrn_sample(reverse, n, is_ctx, qh_ref, lf_ref, vi_ref, qhc_ref, lfc_ref, vic_ref, tri_ref, lv_ref, *rest):
    if reverse:
        of_ref, g_ref, hgn_ref, out_ref, st_ref = rest
    else:
        out_ref, st_ref = rest
    lf = jnp.where(is_ctx, lfc_ref[n], lf_ref[n])
    qh_in = jnp.where(is_ctx, qhc_ref[n], qh_ref[n])
    vi_in = jnp.where(is_ctx, vic_ref[n], vi_ref[n])
    l_hi = lf.astype(BF16)
    r1 = lf - l_hi.astype(F32)
    l_mid = r1.astype(BF16)
    l_lo = (r1 - l_mid.astype(F32)).astype(BF16)
    tri = tri_ref[...]
    b_all = (_dot(tri, l_hi) + _dot(tri, l_mid) + _dot(tri, l_lo)) * LOG2E
    f_all = jnp.exp(lf)
    k_all = 1.0 - f_all
    q_all = qh_in.astype(F32)
    lv = lv_ref[...]
    R = lf.shape[0]
    H = R // 2
    row = lax.broadcasted_iota(jnp.int32, (R, LANES), 0)
    near, far = (0, 1) if not reverse else (1, 0)
    end_row = 0 if reverse else R - 1
    heads = range(HG_HEADS)
    sls = [slice(h * HG_DK, (h + 1) * HG_DK) for h in heads]
    bs = [b_all[:, sl] for sl in sls]
    qs = [q_all[:, sl] for sl in sls]
    ks = [k_all[:, sl] for sl in sls]
    fs = [f_all[:, sl] for sl in sls]
    vs = [vi_in[:, sl] for sl in sls]

    diag = lv == N_LEVELS - 1
    a0 = [jnp.where(diag, _dot_nt(qs[h][:H].astype(BF16), ks[h][:H].astype(BF16)), 0.0) for h in heads]
    a1 = [jnp.where(diag, _dot_nt(qs[h][H:].astype(BF16), ks[h][H:].astype(BF16)), 0.0) for h in heads]
    a_off = [None] * HG_HEADS
    zs = list(bs)
    for lev in range(N_LEVELS):
        m = 1 << lev
        mask = lv == lev
        q_side = ((row >> lev) & 1) == far
        for h in heads:
            bh, qh, kh = bs[h], qs[h], ks[h]
            if lev == 0:
                zs[h] = _replace_bit(zs[h], row, lev, far)
                u = jnp.where(q_side, qh * fs[h], kh).astype(BF16)
            elif lev < N_SMALL_LEVELS:
                bmid = _replace_bit(zs[h], row, lev, near)
                zs[h] = _replace_bit(zs[h], row, lev, far)
                qk = jnp.where(q_side, qh, kh)
                arg = _neg_abs(bh - bmid)
            elif 2 * m == SUBLANES:
                ref = m if reverse else m - 1
                bmid = jnp.concatenate([jnp.broadcast_to(bh[lo + ref:lo + ref + 1], (SUBLANES, HG_DK))
                                        for lo in range(0, R, SUBLANES)], axis=0)
                qk = jnp.where(q_side, qh, kh)
                arg = _neg_abs(bh - bmid)
            else:
                args, parts = [], []
                for j in range(R // (2 * m)):
                    lo = j * 2 * m
                    ref = lo + (m if reverse else m - 1)
                    bref = bh[ref:ref + 1]
                    if reverse:
                        args += [bh[lo:lo + m] - bref, bref - bh[lo + m:lo + 2 * m]]
                        parts += [qh[lo:lo + m], kh[lo + m:lo + 2 * m]]
                    else:
                        args += [bref - bh[lo:lo + m], bh[lo + m:lo + 2 * m] - bref]
                        parts += [kh[lo:lo + m], qh[lo + m:lo + 2 * m]]
                arg = jnp.concatenate(args, axis=0)
                qk = jnp.concatenate(parts, axis=0)
            if lev > 0:
                u = (qk * jnp.exp2(arg)).astype(BF16)
            if lev < N_LEVELS - 1:
                a0[h] = jnp.where(mask, _dot_nt(u[:H], u[:H]), a0[h])
                a1[h] = jnp.where(mask, _dot_nt(u[H:], u[H:]), a1[h])
            elif reverse:
                a_off[h] = _dot_nt(u[:H], u[H:])
            else:
                a_off[h] = _dot_nt(u[H:], u[:H])
        yield True

    for h in heads:
        sl, bh, qh, kh, vh = sls[h], bs[h], qs[h], ks[h], vs[h]
        zend = bh[end_row:end_row + 1]
        st = st_ref[n * HG_HEADS + h]
        if reverse:
            o_in = jnp.concatenate([
                _dot(jnp.concatenate([a0[h], a_off[h]], axis=1).astype(BF16), vh),
                _dot(a1[h].astype(BF16), vh[H:])], axis=0)
        else:
            o_in = jnp.concatenate([
                _dot(a0[h].astype(BF16), vh[:H]),
                _dot(jnp.concatenate([a_off[h], a1[h]], axis=1).astype(BF16), vh)], axis=0)
        o = o_in + _dot_nt((qh * jnp.exp2(bh)).astype(BF16), st.astype(BF16))
        k_end = (kh * jnp.exp2(zend - bh)).astype(BF16)
        st_ref[n * HG_HEADS + h] = jnp.exp2(zend) * st + _dot_tn(vh, k_end)

        if reverse:
            tot = of_ref[n, :, sl] + o
            out_ref[n, :, sl] = (_rms(tot, hgn_ref[...]) * g_ref[n, :, sl].astype(F32)).astype(out_ref.dtype)
        else:
            out_ref[n, :, sl] = o
        yield True


def _hgrn_call(reverse, lat, ctx, tri, lv, extra=()):
    B, T, KW = lat[0].shape
    R = ROW_TILE
    n_lat = T // R

    if reverse:
        def lat_idx(b, i):
            return (b, jnp.where(i == 0, n_lat - 1, n_lat - i), 0)
    else:
        def lat_idx(b, i):
            return (b, jnp.maximum(i - 1, 0), 0)

    NS = HGRN_SAMPLES
    const2 = lambda b, i: (0, 0)
    in_specs = [pl.BlockSpec((NS, R, KW), lat_idx)] * 3 + [pl.BlockSpec((NS, R, KW), lambda b, i: (b, 0, 0))] * 3 + [
        pl.BlockSpec((R, R), const2),
        pl.BlockSpec((R // 2, R // 2), const2),
    ]
    if reverse:
        in_specs += [
            pl.BlockSpec((NS, R, KW), lat_idx),
            pl.BlockSpec((NS, R, KW), lat_idx),
            pl.BlockSpec((1, HG_DV), const2),
        ]
        out_dtype = BF16
    else:
        out_dtype = F32
    return pl.pallas_call(
        functools.partial(_hgrn_kernel, reverse),
        grid=(B // NS, n_lat + 1),
        in_specs=in_specs,
        out_specs=pl.BlockSpec((NS, R, KW), lat_idx),
        out_shape=jax.ShapeDtypeStruct((B, T, KW), out_dtype),
        scratch_shapes=[pltpu.VMEM((NS * HG_HEADS, HG_DV, HG_DK), F32)],
        compiler_params=pltpu.CompilerParams(dimension_semantics=("arbitrary", "arbitrary"),
                                             vmem_limit_bytes=VMEM_LIMIT),
        name="hgrn_bwd" if reverse else "hgrn_fwd",
    )(*lat, *ctx, tri, lv, *extra)


def _attn_kernel(q_ref, k_ref, kc_ref, v_ref, vc_ref, o_ref):
    n_lat = k_ref.shape[1]

    def scores(h):
        sl = slice(h * HEAD_SLOT, (h + 1) * HEAD_SLOT)
        qh = q_ref[0, :, sl]
        return jnp.concatenate([_dot_nt(qh, k_ref[0, :, sl]), _dot_nt(qh, kc_ref[0, :, sl])], axis=1)

    def attend(h, s):
        sl = slice(h * HEAD_SLOT, (h + 1) * HEAD_SLOT)
        pb = jnp.exp2(s - jnp.max(s, axis=-1, keepdims=True)).astype(BF16)
        o = _dot(pb[:, :n_lat], v_ref[0, :, sl]) + _dot(pb[:, n_lat:], vc_ref[0, :, sl])
        return o * (1.0 / o[:, MLA_V:MLA_V + 1])

    outs = []
    s_next = scores(0)
    for h in range(MLA_HEADS):
        s = s_next
        if h + 1 < MLA_HEADS:
            s_next = scores(h + 1)
        outs.append(attend(h, s))
    first = _lane_lt(q_ref.shape[1], MLA_V)
    pairs = [jnp.where(first, outs[2 * j], pltpu.roll(outs[2 * j + 1], MLA_V, 1)) for j in range(MLA_HEADS // 2)]
    o_ref[0] = jnp.concatenate(pairs, axis=-1).astype(o_ref.dtype)


def _attn_call(q, k, k_ctx, v, v_ctx):
    B, T, nq = q.shape
    R = ATTN_ROW_TILE
    whole = lambda a: pl.BlockSpec((1,) + a.shape[1:], lambda b, i: (b, 0, 0))
    return pl.pallas_call(
        _attn_kernel,
        grid=(B, T // R),
        in_specs=[pl.BlockSpec((1, R, nq), lambda b, i: (b, i, 0)), whole(k), whole(k_ctx), whole(v), whole(v_ctx)],
        out_specs=pl.BlockSpec((1, R, MLA_HEADS * MLA_V), lambda b, i: (b, i, 0)),
        out_shape=jax.ShapeDtypeStruct((B, T, MLA_HEADS * MLA_V), BF16),
        compiler_params=pltpu.CompilerParams(dimension_semantics=("arbitrary", "arbitrary"),
                                             vmem_limit_bytes=VMEM_LIMIT),
        name="mla_attn",
    )(q, k, k_ctx, v, v_ctx)


def _mlp_kernel(n_ff_chunks, x_ref, om_ref, oh_ref, mod_ref, woa_ref, wob_ref, nmlp_ref, w1_ref, w2_ref,
                fn_ref, o_ref):
    mod = mod_ref[0]
    y = _dot(om_ref[0], woa_ref[...]) + _dot(oh_ref[0], wob_ref[...])
    x1 = x_ref[0] + mod[0:1] * y
    h2 = (_rms(x1, nmlp_ref[...]) * (1.0 + mod[2:3]) + mod[1:2]).astype(BF16)
    ff = w1_ref.shape[1] // n_ff_chunks
    acc = jnp.zeros(x1.shape, F32)
    for c in range(n_ff_chunks):
        u = jnp.maximum(_dot(h2, w1_ref[:, c * ff:(c + 1) * ff]), 0.0)
        acc = acc + _dot((u * u).astype(BF16), w2_ref[c * ff:(c + 1) * ff, :])
    x2 = x1 + mod[3:4] * acc
    o_ref[0] = _rms(x2, fn_ref[...])


def _mlp_call(x, o_mla, o_hg, mod4, woa, wob, nmlp, w1, w2, fnorm):
    B, T, D = x.shape
    R = MLP_ROW_TILE
    dff = w1.shape[1]
    half = o_mla.shape[2]
    row_idx = lambda b, i: (b, i, 0)
    const2 = lambda b, i: (0, 0)
    return pl.pallas_call(
        functools.partial(_mlp_kernel, 4),
        grid=(B, T // R),
        in_specs=[
            pl.BlockSpec((1, R, D), row_idx),
            pl.BlockSpec((1, R, half), row_idx),
            pl.BlockSpec((1, R, half), row_idx),
            pl.BlockSpec((1, 4, D), lambda b, i: (b, 0, 0)),
            pl.BlockSpec((half, D), const2, pipeline_mode=pl.Buffered(1)),
            pl.BlockSpec((half, D), const2, pipeline_mode=pl.Buffered(1)),
            pl.BlockSpec((1, D), const2),
            pl.BlockSpec((D, dff), const2, pipeline_mode=pl.Buffered(1)),
            pl.BlockSpec((dff, D), const2, pipeline_mode=pl.Buffered(1)),
            pl.BlockSpec((1, D), const2),
        ],
        out_specs=pl.BlockSpec((1, R, D), row_idx),
        out_shape=jax.ShapeDtypeStruct((B, T, D), F32),
        compiler_params=pltpu.CompilerParams(dimension_semantics=("arbitrary", "arbitrary"),
                                             vmem_limit_bytes=VMEM_LIMIT),
        name="out_mlp",
    )(x, o_mla, o_hg, mod4, woa, wob, nmlp, w1, w2, fnorm)


def _rope_slot(w):
    z = jnp.zeros(w.shape[:-1] + (MLA_NOPE,), w.dtype)
    z2 = jnp.zeros(w.shape[:-1] + (HEAD_SLOT - MLA_NOPE - MLA_ROPE,), w.dtype)
    return jnp.concatenate([z, w, z2], axis=-1)


def _pack_w_in(w_in):
    w = w_in.astype(BF16)
    n_mla = MLA_Q_RANK + MLA_KV_RANK
    return jnp.concatenate([w[:, :n_mla], _rope_slot(w[:, n_mla:n_mla + MLA_ROPE]), w[:, n_mla + MLA_ROPE:]], axis=1)


def _pack_w_uq(w_uq):
    k = w_uq.shape[0]
    w3 = w_uq.reshape(k, MLA_HEADS, MLA_NOPE + MLA_ROPE)
    nope = w3[..., :MLA_NOPE].reshape(k, MLA_HEADS * MLA_NOPE)
    rope = w3[..., MLA_NOPE:].reshape(k, MLA_HEADS // 2, 2 * MLA_ROPE)
    rope = jnp.concatenate([jnp.zeros((k, MLA_HEADS // 2, LANES - 2 * MLA_ROPE), w_uq.dtype), rope], axis=-1)
    return jnp.concatenate([nope, rope.reshape(k, (MLA_HEADS // 2) * LANES)], axis=1).astype(BF16)


def _pack_w_ukv(w_ukv):
    k = w_ukv.shape[0]
    w3 = w_ukv.reshape(k, MLA_HEADS, MLA_NOPE + MLA_V)
    kk = w3[..., :MLA_NOPE].reshape(k, MLA_HEADS * MLA_NOPE)
    vv = w3[..., MLA_NOPE:].reshape(k, MLA_HEADS * MLA_V)
    return jnp.concatenate([kk, vv], axis=1).astype(BF16)


def _rope_tables(T):
    rows = T // GRID_W
    row = np.repeat(np.arange(rows), GRID_W).astype(np.float64)
    col = np.tile(np.arange(GRID_W), rows).astype(np.float64)
    n = MLA_ROPE // 4
    inv = ROPE_BASE ** (-np.arange(n, dtype=np.float64) / n)
    ar, ac = row[:, None] * inv, col[:, None] * inv
    cos32 = np.concatenate([np.cos(ar), np.cos(ar), np.cos(ac), np.cos(ac)], axis=-1)
    sin32 = np.concatenate([np.sin(ar), np.sin(ar), np.sin(ac), np.sin(ac)], axis=-1)
    cos_t = np.concatenate([np.ones((T, MLA_NOPE)), cos32, cos32], axis=-1)
    sin_t = np.concatenate([np.zeros((T, MLA_NOPE)), sin32, sin32], axis=-1)
    return jnp.asarray(cos_t, F32), jnp.asarray(sin_t, F32)


def _scan_constants(reverse):
    R = ROW_TILE
    t = np.arange(R, dtype=np.int32)[:, None]
    s = np.arange(R, dtype=np.int32)[None, :]
    if reverse:
        t, s = s, t
    tri = (t >= s)
    H = R // 2
    t, s = np.broadcast_to(t, (R, R))[:H, :H], np.broadcast_to(s, (R, R))[:H, :H]
    x = t ^ s
    lev = np.zeros((H, H), np.int32)
    for bit in range(1, N_LEVELS - 1):
        lev = np.where((x >> bit) > 0, bit, lev)
    lv = np.where(t > s, lev, np.where(t == s, N_LEVELS - 1, -1))
    return jnp.asarray(tri, BF16), jnp.asarray(lv, jnp.int32)


def kernel(x, c, ctx, c_ctx, w_ada, b_ada, norm_mix, w_in, q_norm, w_uq, kv_norm, w_ukv, hgrn_lb, hgrn_norm,
           w_out, norm_mlp, w_mlp_in, w_mlp_out, final_norm):
    B, T, D = x.shape
    L = ctx.shape[1]
    assert w_ada.shape[0] == 1, "single-layer block"
    assert T % ROW_TILE == 0 and L == ROW_TILE and T % GRID_W == 0
    assert T % MLP_ROW_TILE == 0 and T % ATTN_ROW_TILE == 0 and T % PROJ_ROW_TILE == 0
    assert PROJ_ROW_TILE % L == 0 and B % (PROJ_ROW_TILE // L) == 0 and T % L == 0 and B % HGRN_SAMPLES == 0

    pad_rows = (-(B + 1)) % 8
    cc = jnp.concatenate([c, c_ctx[None, :], jnp.zeros((pad_rows, D), F32)], axis=0)
    mod = _ada_call(cc, w_ada[0], b_ada[0][None, :])
    mod_lat = mod[:B].reshape(B, 6, D)
    mod_ctx = mod[B].reshape(1, 6, D)
    mod4 = mod_lat[:, 2:6]

    cos_t, sin_t = _rope_tables(T)
    win_p, wkv_p = _pack_w_in(w_in[0]), _pack_w_ukv(w_ukv[0])
    q, k, v, qh, lff, lfb, vi, g = _inproj_lat_call(
        x, mod_lat[:, 0:2], norm_mix, win_p, q_norm, _pack_w_uq(w_uq[0]), kv_norm, wkv_p, hgrn_lb, cos_t, sin_t)
    kc, vc, qhc, lffc, lfbc, vic = _inproj_ctx_call(
        ctx, mod_ctx[:, 0:2], norm_mix, win_p, kv_norm, wkv_p, hgrn_lb)

    tri_f, lv_f = _scan_constants(False)
    tri_b, lv_b = _scan_constants(True)
    o_f = _hgrn_call(False, (qh, lff, vi), (qhc, lffc, vic), tri_f, lv_f)
    o_hg = _hgrn_call(True, (qh, lfb, vi), (qhc, lfbc, vic), tri_b, lv_b, extra=(o_f, g, hgrn_norm))

    o_mla = _attn_call(q, k, kc, v, vc)

    half = MLA_HEADS * MLA_V
    wo = w_out[0].astype(BF16)
    return _mlp_call(x, o_mla, o_hg, mod4, wo[:half], wo[half:], norm_mlp, w_mlp_in[0].astype(BF16),
                     w_mlp_out[0].astype(BF16), final_norm[None, :])
```

```python
import functools

import jax
import jax.numpy as jnp
import numpy as np
from jax import lax
from jax.experimental import pallas as pl
from jax.experimental.pallas import tpu as pltpu

F32 = jnp.float32
BF16 = jnp.bfloat16

GRID_W = 64
MLA_HEADS = 8
MLA_Q_RANK = 256
MLA_KV_RANK = 128
MLA_NOPE = 64
MLA_ROPE = 32
MLA_V = 64
HG_HEADS = 4
HG_DK = 128
HG_DV = 128
HG_KW = HG_HEADS * HG_DK
ROPE_BASE = 10000.0
EPS = 1e-6
LOG2E = 1.4426950408889634

LANES = 128
SUBLANES = 8
HEAD_SLOT = LANES
ROW_TILE = 256
HGRN_SAMPLES = 4
MLP_ROW_TILE = 512
MLP_FF_CHUNKS = 4
ADA_COL_TILE = 1024
F32_SIGN_BIT = 0x80000000
PROJ_ROW_TILE = 512
ATTN_ROW_TILE = 512
VMEM_LIMIT = 56 * 1024 * 1024

_C_CQ = (0, 256)
_C_CKV = (256, 384)
_C_KR = (384, 512)
_C_HQ = (512, 1024)
_C_HF = (1024, 1536)
_C_HB = (1536, 2048)
_C_HI = (2048, 2560)
_C_HG = (2560, 3072)
IN_P_WIDTH = 3072


def _rms(x, g):
    ms = jnp.mean(x * x, axis=-1, keepdims=True)
    return (x * lax.rsqrt(ms + EPS)) * g


def _silu(x):
    return x * jax.nn.sigmoid(x)


def _dot(a, b):
    return jnp.dot(a, b, preferred_element_type=F32)


def _dot_nt(a, b):
    return lax.dot_general(a, b, (((1,), (1,)), ((), ())), preferred_element_type=F32)


def _dot_tn(a, b):
    return lax.dot_general(a, b, (((0,), (0,)), ((), ())), preferred_element_type=F32)


def _ada_kernel(c_ref, w_ref, b_ref, o_ref):
    s = _silu(c_ref[...])
    o_ref[...] = jnp.dot(s, w_ref[...], preferred_element_type=F32,
                         precision=lax.Precision.HIGHEST) + b_ref[...]


def _ada_call(cc, w_ada, b_ada):
    rows, d = cc.shape
    n = w_ada.shape[1]
    tn = ADA_COL_TILE
    return pl.pallas_call(
        _ada_kernel,
        grid=(n // tn,),
        in_specs=[
            pl.BlockSpec((rows, d), lambda j: (0, 0)),
            pl.BlockSpec((d, tn), lambda j: (0, j)),
            pl.BlockSpec((1, tn), lambda j: (0, j)),
        ],
        out_specs=pl.BlockSpec((rows, tn), lambda j: (0, j)),
        out_shape=jax.ShapeDtypeStruct((rows, n), F32),
        compiler_params=pltpu.CompilerParams(dimension_semantics=("arbitrary",), vmem_limit_bytes=VMEM_LIMIT),
        name="ada_mod",
    )(cc, w_ada, b_ada)


def _lane_lt(n_rows, bound):
    return lax.broadcasted_iota(jnp.int32, (n_rows, LANES), 1) < bound


def _rotate_half_lanes(x):
    lane = lax.broadcasted_iota(jnp.int32, x.shape, 1)
    n = MLA_ROPE // 4
    return jnp.where((lane & (2 * n - 1)) < n, -pltpu.roll(x, LANES - n, 1), pltpu.roll(x, n, 1))


def _head_slots(nope, rope_lo, rope_hi):
    first = _lane_lt(nope.shape[0], MLA_NOPE)
    return (jnp.where(first, nope, rope_lo), jnp.where(first, pltpu.roll(nope, MLA_NOPE, 1), rope_hi))


def _inproj_kernel(is_ctx, *refs):
    if is_ctx:
        (x_ref, mod_ref, nmix_ref, win_ref, kvn_ref, wkv_ref, lb_ref,
         k_out, v_out, lff_out, lfb_out, vi_out) = refs
    else:
        (x_ref, mod_ref, nmix_ref, win_ref, qn_ref, wq_ref, kvn_ref, wkv_ref, lb_ref, cos_ref, sin_ref,
         q_out, k_out, v_out, qh_out, lff_out, lfb_out, vi_out, g_out) = refs
    mod = mod_ref[0]
    h = _rms(x_ref[0], nmix_ref[...]) * (1.0 + mod[1:2]) + mod[0:1]
    hb = h.astype(BF16)
    n_pairs = MLA_HEADS // 2
    half = n_pairs * LANES

    def proj(c):
        return _dot(hb, win_ref[:, c[0]:c[1]])

    def put(ref, val):
        ref[...] = val.astype(ref.dtype).reshape(ref.shape)

    def pair(a, j):
        return a[:, j * LANES:(j + 1) * LANES]

    lbr = lb_ref[...]
    e = jnp.exp(lbr - jnp.max(lbr, axis=0, keepdims=True))
    lb = e[0] / jnp.sum(e, axis=0)
    lbf = lb[0:1]
    lbb = lb[1:2]

    ckv = proj(_C_CKV)
    kr = proj(_C_KR)
    if not is_ctx:
        cq = proj(_C_CQ)
    put(lff_out, jnp.log(lbf + (1.0 - lbf) * jax.nn.sigmoid(proj(_C_HF))))
    put(vi_out, proj(_C_HI))
    kv = _dot(_rms(ckv, kvn_ref[...]).astype(BF16), wkv_ref[...])
    put(lfb_out, jnp.log(lbb + (1.0 - lbb) * jax.nn.sigmoid(proj(_C_HB))))
    if not is_ctx:
        cos = cos_ref[...]
        sin = sin_ref[...]
        kr = kr * cos + _rotate_half_lanes(kr) * sin
        qq = _dot(_rms(cq, qn_ref[...]).astype(BF16), wq_ref[...])
    if not is_ctx:
        put(qh_out, _silu(proj(_C_HQ)))
    k_slots = []
    for j in range(n_pairs):
        k_slots += _head_slots(pair(kv, j), kr, kr)
    put(k_out, jnp.concatenate(k_slots, axis=1))
    lane = lax.broadcasted_iota(jnp.int32, (kv.shape[0], LANES), 1)
    ones_lane = jnp.where(lane == MLA_V, 1.0, 0.0)
    v_slots = []
    for j in range(n_pairs):
        v_slots += _head_slots(pair(kv, n_pairs + j), ones_lane, ones_lane)
    put(v_out, jnp.concatenate(v_slots, axis=1))
    if not is_ctx:
        scale = (MLA_NOPE + MLA_ROPE) ** -0.5 * LOG2E
        q_slots = []
        for j in range(n_pairs):
            r = pair(qq, n_pairs + j)
            r = (r * cos + _rotate_half_lanes(r) * sin) * scale
            q_slots += _head_slots(pair(qq, j) * scale, r, pltpu.roll(r, LANES - MLA_ROPE, 1))
        put(q_out, jnp.concatenate(q_slots, axis=1))
        put(g_out, _silu(proj(_C_HG)))


def _inproj_out_shapes(B, rows, latent):
    nq = MLA_HEADS * HEAD_SLOT
    kv = [((B, rows, nq), BF16),
          ((B, rows, nq), BF16)]
    scan = [((B, rows, HG_KW), F32),
            ((B, rows, HG_KW), F32),
            ((B, rows, HG_KW), BF16)]
    if not latent:
        return kv + scan
    return [((B, rows, nq), BF16)] + kv + [((B, rows, HG_KW), BF16)] + scan + [((B, rows, HG_KW), BF16)]


def _inproj_lat_call(x, mod_lat, nmix, win_p, qn, wq_p, kvn, wkv_p, lb_raw, cos_t, sin_t):
    B, T, D = x.shape
    R = PROJ_ROW_TILE
    nq = MLA_HEADS * HEAD_SLOT
    row_idx = lambda b, i: (b, i, 0)
    const2 = lambda b, i: (0, 0)
    in_specs = [
        pl.BlockSpec((1, R, D), row_idx),
        pl.BlockSpec((1, 2, D), lambda b, i: (b, 0, 0)),
        pl.BlockSpec((1, D), const2),
        pl.BlockSpec((D, IN_P_WIDTH), const2, pipeline_mode=pl.Buffered(1)),
        pl.BlockSpec((1, MLA_Q_RANK), const2),
        pl.BlockSpec((MLA_Q_RANK, nq), const2, pipeline_mode=pl.Buffered(1)),
        pl.BlockSpec((1, MLA_KV_RANK), const2),
        pl.BlockSpec((MLA_KV_RANK, nq), const2, pipeline_mode=pl.Buffered(1)),
        pl.BlockSpec(lb_raw.shape, lambda b, i: (0, 0, 0)),
        pl.BlockSpec((R, LANES), lambda b, i: (i, 0)),
        pl.BlockSpec((R, LANES), lambda b, i: (i, 0)),
    ]
    shapes = _inproj_out_shapes(B, T, True)
    return pl.pallas_call(
        functools.partial(_inproj_kernel, False),
        grid=(B, T // R),
        in_specs=in_specs,
        out_specs=[pl.BlockSpec((1, R, s[2]), row_idx) for s, _ in shapes],
        out_shape=[jax.ShapeDtypeStruct(s, d) for s, d in shapes],
        compiler_params=pltpu.CompilerParams(dimension_semantics=("arbitrary", "arbitrary"),
                                             vmem_limit_bytes=VMEM_LIMIT),
        name="inproj_lat",
    )(x, mod_lat, nmix, win_p, qn, wq_p, kvn, wkv_p, lb_raw, cos_t, sin_t)


def _inproj_ctx_call(ctx, mod_ctx, nmix, win_p, kvn, wkv_p, lb_raw):
    B, L, D = ctx.shape
    R = PROJ_ROW_TILE
    per = R // L
    nq = MLA_HEADS * HEAD_SLOT
    x2 = ctx.reshape(B // per, R, D)
    const2 = lambda i: (0, 0)
    in_specs = [
        pl.BlockSpec((1, R, D), lambda i: (i, 0, 0)),
        pl.BlockSpec((1, 2, D), lambda i: (0, 0, 0)),
        pl.BlockSpec((1, D), const2),
        pl.BlockSpec((D, IN_P_WIDTH), const2, pipeline_mode=pl.Buffered(1)),
        pl.BlockSpec((1, MLA_KV_RANK), const2),
        pl.BlockSpec((MLA_KV_RANK, nq), const2, pipeline_mode=pl.Buffered(1)),
        pl.BlockSpec(lb_raw.shape, lambda i: (0, 0, 0)),
    ]
    shapes = _inproj_out_shapes(B, L, False)
    return pl.pallas_call(
        functools.partial(_inproj_kernel, True),
        grid=(B // per,),
        in_specs=in_specs,
        out_specs=[pl.BlockSpec((per, L, s[2]), lambda i: (i, 0, 0)) for s, _ in shapes],
        out_shape=[jax.ShapeDtypeStruct(s, d) for s, d in shapes],
        compiler_params=pltpu.CompilerParams(dimension_semantics=("arbitrary",), vmem_limit_bytes=VMEM_LIMIT),
        name="inproj_ctx",
    )(x2, mod_ctx, nmix, win_p, kvn, wkv_p, lb_raw)


N_LEVELS = 8
N_SMALL_LEVELS = 2


def _neg_abs(x):
    bits = lax.bitcast_convert_type(x, jnp.uint32) | jnp.uint32(F32_SIGN_BIT)
    return lax.bitcast_convert_type(bits, F32)


def _replace_bit(y, row, bit, value):
    step = 1 << bit
    n = y.shape[0]
    has = ((row >> bit) & 1) == 1
    if value == 1:
        return jnp.where(has, y, pltpu.roll(y, n - step, 0))
    return jnp.where(has, pltpu.roll(y, step, 0), y)


def _cum_log2_decay(lf, tri):
    l_hi = lf.astype(BF16)
    r1 = lf - l_hi.astype(F32)
    l_mid = r1.astype(BF16)
    l_lo = (r1 - l_mid.astype(F32)).astype(BF16)
    return (_dot(tri, l_hi) + _dot(tri, l_mid) + _dot(tri, l_lo)) * LOG2E


def _hgrn_ctx_kernel(lff_ref, lfb_ref, vi_ref, trif_ref, trib_ref, stf_ref, stb_ref):
    R = vi_ref.shape[1]
    for n in range(vi_ref.shape[0]):
        for reverse, lf_ref, tri_ref, st_ref in ((False, lff_ref, trif_ref, stf_ref), (True, lfb_ref, trib_ref, stb_ref)):
            lf = lf_ref[n]
            b = _cum_log2_decay(lf, tri_ref[...])
            end_row = 0 if reverse else R - 1
            k_end = ((1.0 - jnp.exp(lf)) * jnp.exp2(b[end_row:end_row + 1] - b)).astype(BF16)
            for h in range(HG_HEADS):
                sl = slice(h * HG_DK, (h + 1) * HG_DK)
                st_ref[n, h] = _dot_tn(vi_ref[n, :, sl], k_end[:, sl])


def _hgrn_ctx_call(lff_c, lfb_c, vi_c, tri_f, tri_b):
    B, R, KW = vi_c.shape
    NS = HGRN_SAMPLES
    rows = pl.BlockSpec((NS, R, KW), lambda b: (b, 0, 0))
    const2 = pl.BlockSpec((R, R), lambda b: (0, 0))
    st_spec = pl.BlockSpec((NS, HG_HEADS, HG_DV, HG_DK), lambda b: (b, 0, 0, 0))
    st_shape = jax.ShapeDtypeStruct((B, HG_HEADS, HG_DV, HG_DK), F32)
    return pl.pallas_call(
        _hgrn_ctx_kernel,
        grid=(B // NS,),
        in_specs=[rows, rows, rows, const2, const2],
        out_specs=[st_spec, st_spec],
        out_shape=[st_shape, st_shape],
        compiler_params=pltpu.CompilerParams(dimension_semantics=("arbitrary",), vmem_limit_bytes=VMEM_LIMIT),
        name="hgrn_ctx",
    )(lff_c, lfb_c, vi_c, tri_f, tri_b)


def _hgrn_kernel(reverse, *refs):
    st0_ref, st_ref = refs[3], refs[-1]

    @pl.when(pl.program_id(1) == 0)
    def _():
        st_ref[...] = st0_ref[...].reshape(st_ref.shape)

    stages = [_hgrn_sample(reverse, n, *refs) for n in range(refs[0].shape[0])]
    while stages:
        stages = [s for s in stages if next(s, False)]


def _hgrn_sample(reverse, n, qh_ref, lf_ref, vi_ref, st0_ref, tri_ref, lv_ref, *rest):
    if reverse:
        of_ref, g_ref, hgn_ref, out_ref, st_ref = rest
    else:
        out_ref, st_ref = rest
    lf = lf_ref[n]
    vi_in = vi_ref[n]
    b_all = _cum_log2_decay(lf, tri_ref[...])
    f_all = jnp.exp(lf)
    k_all = 1.0 - f_all
    q_all = qh_ref[n].astype(F32)
    lv = lv_ref[...]
    R = lf.shape[0]
    H = R // 2
    row = lax.broadcasted_iota(jnp.int32, (R, LANES), 0)
    near, far = (0, 1) if not reverse else (1, 0)
    end_row = 0 if reverse else R - 1
    heads = range(HG_HEADS)
    sls = [slice(h * HG_DK, (h + 1) * HG_DK) for h in heads]
    bs = [b_all[:, sl] for sl in sls]
    qs = [q_all[:, sl] for sl in sls]
    ks = [k_all[:, sl] for sl in sls]
    fs = [f_all[:, sl] for sl in sls]
    vs = [vi_in[:, sl] for sl in sls]

    diag = lv == N_LEVELS - 1
    a0 = [jnp.where(diag, _dot_nt(qs[h][:H].astype(BF16), ks[h][:H].astype(BF16)), 0.0) for h in heads]
    a1 = [jnp.where(diag, _dot_nt(qs[h][H:].astype(BF16), ks[h][H:].astype(BF16)), 0.0) for h in heads]
    a_off = [None] * HG_HEADS
    zs = list(bs)
    for lev in range(N_LEVELS):
        m = 1 << lev
        mask = lv == lev
        q_side = ((row >> lev) & 1) == far
        for h in heads:
            bh, qh, kh = bs[h], qs[h], ks[h]
            if lev == 0:
                zs[h] = _replace_bit(zs[h], row, lev, far)
                u = jnp.where(q_side, qh * fs[h], kh).astype(BF16)
            elif lev < N_SMALL_LEVELS:
                bmid = _replace_bit(zs[h], row, lev, near)
                zs[h] = _replace_bit(zs[h], row, lev, far)
                qk = jnp.where(q_side, qh, kh)
                arg = _neg_abs(bh - bmid)
            elif 2 * m == SUBLANES:
                ref = m if reverse else m - 1
                bmid = jnp.concatenate([jnp.broadcast_to(bh[lo + ref:lo + ref + 1], (SUBLANES, HG_DK))
                                        for lo in range(0, R, SUBLANES)], axis=0)
                qk = jnp.where(q_side, qh, kh)
                arg = _neg_abs(bh - bmid)
            else:
                args, parts = [], []
                for j in range(R // (2 * m)):
                    lo = j * 2 * m
                    ref = lo + (m if reverse else m - 1)
                    bref = bh[ref:ref + 1]
                    if reverse:
                        args += [bh[lo:lo + m] - bref, bref - bh[lo + m:lo + 2 * m]]
                        parts += [qh[lo:lo + m], kh[lo + m:lo + 2 * m]]
                    else:
                        args += [bref - bh[lo:lo + m], bh[lo + m:lo + 2 * m] - bref]
                        parts += [kh[lo:lo + m], qh[lo + m:lo + 2 * m]]
                arg = jnp.concatenate(args, axis=0)
                qk = jnp.concatenate(parts, axis=0)
            if lev > 0:
                u = (qk * jnp.exp2(arg)).astype(BF16)
            if lev < N_LEVELS - 1:
                a0[h] = jnp.where(mask, _dot_nt(u[:H], u[:H]), a0[h])
                a1[h] = jnp.where(mask, _dot_nt(u[H:], u[H:]), a1[h])
            elif reverse:
                a_off[h] = _dot_nt(u[:H], u[H:])
            else:
                a_off[h] = _dot_nt(u[H:], u[:H])
        yield True

    for h in heads:
        sl, bh, qh, kh, vh = sls[h], bs[h], qs[h], ks[h], vs[h]
        zend = bh[end_row:end_row + 1]
        st = st_ref[n * HG_HEADS + h]
        if reverse:
            o_in = jnp.concatenate([
                _dot(jnp.concatenate([a0[h], a_off[h]], axis=1).astype(BF16), vh),
                _dot(a1[h].astype(BF16), vh[H:])], axis=0)
        else:
            o_in = jnp.concatenate([
                _dot(a0[h].astype(BF16), vh[:H]),
                _dot(jnp.concatenate([a_off[h], a1[h]], axis=1).astype(BF16), vh)], axis=0)
        o = o_in + _dot_nt((qh * jnp.exp2(bh)).astype(BF16), st.astype(BF16))
        k_end = (kh * jnp.exp2(zend - bh)).astype(BF16)
        st_ref[n * HG_HEADS + h] = jnp.exp2(zend) * st + _dot_tn(vh, k_end)

        if reverse:
            tot = of_ref[n, :, sl] + o
            out_ref[n, :, sl] = (_rms(tot, hgn_ref[...]) * g_ref[n, :, sl].astype(F32)).astype(out_ref.dtype)
        else:
            out_ref[n, :, sl] = o
        yield True


def _hgrn_call(reverse, lat, st0, tri, lv, extra=()):
    B, T, KW = lat[0].shape
    R = ROW_TILE
    n_lat = T // R

    if reverse:
        def lat_idx(b, i):
            return (b, n_lat - 1 - i, 0)
    else:
        def lat_idx(b, i):
            return (b, i, 0)

    NS = HGRN_SAMPLES
    const2 = lambda b, i: (0, 0)
    in_specs = [pl.BlockSpec((NS, R, KW), lat_idx)] * 3 + [
        pl.BlockSpec((NS, HG_HEADS, HG_DV, HG_DK), lambda b, i: (b, 0, 0, 0)),
        pl.BlockSpec((R, R), const2),
        pl.BlockSpec((R // 2, R // 2), const2),
    ]
    if reverse:
        in_specs += [
            pl.BlockSpec((NS, R, KW), lat_idx),
            pl.BlockSpec((NS, R, KW), lat_idx),
            pl.BlockSpec((1, HG_DV), const2),
        ]
        out_dtype = BF16
    else:
        out_dtype = F32
    return pl.pallas_call(
        functools.partial(_hgrn_kernel, reverse),
        grid=(B // NS, n_lat),
        in_specs=in_specs,
        out_specs=pl.BlockSpec((NS, R, KW), lat_idx),
        out_shape=jax.ShapeDtypeStruct((B, T, KW), out_dtype),
        scratch_shapes=[pltpu.VMEM((NS * HG_HEADS, HG_DV, HG_DK), F32)],
        compiler_params=pltpu.CompilerParams(dimension_semantics=("arbitrary", "arbitrary"),
                                             vmem_limit_bytes=VMEM_LIMIT),
        name="hgrn_bwd" if reverse else "hgrn_fwd",
    )(*lat, st0, tri, lv, *extra)


def _attn_kernel(q_ref, k_ref, kc_ref, v_ref, vc_ref, o_ref):
    n_lat = k_ref.shape[1]

    def scores(h):
        sl = slice(h * HEAD_SLOT, (h + 1) * HEAD_SLOT)
        qh = q_ref[0, :, sl]
        return jnp.concatenate([_dot_nt(qh, k_ref[0, :, sl]), _dot_nt(qh, kc_ref[0, :, sl])], axis=1)

    def attend(h, s):
        sl = slice(h * HEAD_SLOT, (h + 1) * HEAD_SLOT)
        pb = jnp.exp2(s - jnp.max(s, axis=-1, keepdims=True)).astype(BF16)
        o = _dot(pb[:, :n_lat], v_ref[0, :, sl]) + _dot(pb[:, n_lat:], vc_ref[0, :, sl])
        return o * (1.0 / o[:, MLA_V:MLA_V + 1])

    outs = []
    s_next = scores(0)
    for h in range(MLA_HEADS):
        s = s_next
        if h + 1 < MLA_HEADS:
            s_next = scores(h + 1)
        outs.append(attend(h, s))
    first = _lane_lt(q_ref.shape[1], MLA_V)
    pairs = [jnp.where(first, outs[2 * j], pltpu.roll(outs[2 * j + 1], MLA_V, 1)) for j in range(MLA_HEADS // 2)]
    o_ref[0] = jnp.concatenate(pairs, axis=-1).astype(o_ref.dtype)


def _attn_call(q, k, k_ctx, v, v_ctx):
    B, T, nq = q.shape
    R = ATTN_ROW_TILE
    whole = lambda a: pl.BlockSpec((1,) + a.shape[1:], lambda b, i: (b, 0, 0))
    return pl.pallas_call(
        _attn_kernel,
        grid=(B, T // R),
        in_specs=[pl.BlockSpec((1, R, nq), lambda b, i: (b, i, 0)), whole(k), whole(k_ctx), whole(v), whole(v_ctx)],
        out_specs=pl.BlockSpec((1, R, MLA_HEADS * MLA_V), lambda b, i: (b, i, 0)),
        out_shape=jax.ShapeDtypeStruct((B, T, MLA_HEADS * MLA_V), BF16),
        compiler_params=pltpu.CompilerParams(dimension_semantics=("arbitrary", "arbitrary"),
                                             vmem_limit_bytes=VMEM_LIMIT),
        name="mla_attn",
    )(q, k, k_ctx, v, v_ctx)


def _mlp_kernel(n_ff_chunks, x_ref, om_ref, oh_ref, mod_ref, woa_ref, wob_ref, nmlp_ref, w1_ref, w2_ref,
                fn_ref, o_ref):
    mod = mod_ref[0]
    y = _dot(om_ref[0], woa_ref[...]) + _dot(oh_ref[0], wob_ref[...])
    x1 = x_ref[0] + mod[0:1] * y
    h2 = (_rms(x1, nmlp_ref[...]) * (1.0 + mod[2:3]) + mod[1:2]).astype(BF16)
    ff = w1_ref.shape[1] // n_ff_chunks
    acc = jnp.zeros(x1.shape, F32)
    for c in range(n_ff_chunks):
        u = jnp.maximum(_dot(h2, w1_ref[:, c * ff:(c + 1) * ff]), 0.0)
        acc = acc + _dot((u * u).astype(BF16), w2_ref[c * ff:(c + 1) * ff, :])
    x2 = x1 + mod[3:4] * acc
    o_ref[0] = _rms(x2, fn_ref[...])


def _mlp_call(x, o_mla, o_hg, mod4, woa, wob, nmlp, w1, w2, fnorm):
    B, T, D = x.shape
    R = MLP_ROW_TILE
    dff = w1.shape[1]
    half = o_mla.shape[2]
    row_idx = lambda b, i: (b, i, 0)
    const2 = lambda b, i: (0, 0)
    return pl.pallas_call(
        functools.partial(_mlp_kernel, MLP_FF_CHUNKS),
        grid=(B, T // R),
        in_specs=[
            pl.BlockSpec((1, R, D), row_idx),
            pl.BlockSpec((1, R, half), row_idx),
            pl.BlockSpec((1, R, half), row_idx),
            pl.BlockSpec((1, 4, D), lambda b, i: (b, 0, 0)),
            pl.BlockSpec((half, D), const2, pipeline_mode=pl.Buffered(1)),
            pl.BlockSpec((half, D), const2, pipeline_mode=pl.Buffered(1)),
            pl.BlockSpec((1, D), const2),
            pl.BlockSpec((D, dff), const2, pipeline_mode=pl.Buffered(1)),
            pl.BlockSpec((dff, D), const2, pipeline_mode=pl.Buffered(1)),
            pl.BlockSpec((1, D), const2),
        ],
        out_specs=pl.BlockSpec((1, R, D), row_idx),
        out_shape=jax.ShapeDtypeStruct((B, T, D), F32),
        compiler_params=pltpu.CompilerParams(dimension_semantics=("arbitrary", "arbitrary"),
                                             vmem_limit_bytes=VMEM_LIMIT),
        name="out_mlp",
    )(x, o_mla, o_hg, mod4, woa, wob, nmlp, w1, w2, fnorm)


def _rope_slot(w):
    z = jnp.zeros(w.shape[:-1] + (MLA_NOPE,), w.dtype)
    z2 = jnp.zeros(w.shape[:-1] + (HEAD_SLOT - MLA_NOPE - MLA_ROPE,), w.dtype)
    return jnp.concatenate([z, w, z2], axis=-1)


def _pack_w_in(w_in):
    w = w_in.astype(BF16)
    n_mla = MLA_Q_RANK + MLA_KV_RANK
    return jnp.concatenate([w[:, :n_mla], _rope_slot(w[:, n_mla:n_mla + MLA_ROPE]), w[:, n_mla + MLA_ROPE:]], axis=1)


def _pack_w_uq(w_uq):
    k = w_uq.shape[0]
    w3 = w_uq.reshape(k, MLA_HEADS, MLA_NOPE + MLA_ROPE)
    nope = w3[..., :MLA_NOPE].reshape(k, MLA_HEADS * MLA_NOPE)
    rope = w3[..., MLA_NOPE:].reshape(k, MLA_HEADS // 2, 2 * MLA_ROPE)
    rope = jnp.concatenate([jnp.zeros((k, MLA_HEADS // 2, LANES - 2 * MLA_ROPE), w_uq.dtype), rope], axis=-1)
    return jnp.concatenate([nope, rope.reshape(k, (MLA_HEADS // 2) * LANES)], axis=1).astype(BF16)


def _pack_w_ukv(w_ukv):
    k = w_ukv.shape[0]
    w3 = w_ukv.reshape(k, MLA_HEADS, MLA_NOPE + MLA_V)
    kk = w3[..., :MLA_NOPE].reshape(k, MLA_HEADS * MLA_NOPE)
    vv = w3[..., MLA_NOPE:].reshape(k, MLA_HEADS * MLA_V)
    return jnp.concatenate([kk, vv], axis=1).astype(BF16)


def _rope_tables(T):
    rows = T // GRID_W
    row = np.repeat(np.arange(rows), GRID_W).astype(np.float64)
    col = np.tile(np.arange(GRID_W), rows).astype(np.float64)
    n = MLA_ROPE // 4
    inv = ROPE_BASE ** (-np.arange(n, dtype=np.float64) / n)
    ar, ac = row[:, None] * inv, col[:, None] * inv
    cos32 = np.concatenate([np.cos(ar), np.cos(ar), np.cos(ac), np.cos(ac)], axis=-1)
    sin32 = np.concatenate([np.sin(ar), np.sin(ar), np.sin(ac), np.sin(ac)], axis=-1)
    cos_t = np.concatenate([np.ones((T, MLA_NOPE)), cos32, cos32], axis=-1)
    sin_t = np.concatenate([np.zeros((T, MLA_NOPE)), sin32, sin32], axis=-1)
    return jnp.asarray(cos_t, F32), jnp.asarray(sin_t, F32)


def _scan_constants(reverse):
    R = ROW_TILE
    t = np.arange(R, dtype=np.int32)[:, None]
    s = np.arange(R, dtype=np.int32)[None, :]
    if reverse:
        t, s = s, t
    tri = (t >= s)
    H = R // 2
    t, s = np.broadcast_to(t, (R, R))[:H, :H], np.broadcast_to(s, (R, R))[:H, :H]
    x = t ^ s
    lev = np.zeros((H, H), np.int32)
    for bit in range(1, N_LEVELS - 1):
        lev = np.where((x >> bit) > 0, bit, lev)
    lv = np.where(t > s, lev, np.where(t == s, N_LEVELS - 1, -1))
    return jnp.asarray(tri, BF16), jnp.asarray(lv, jnp.int32)


def kernel(x, c, ctx, c_ctx, w_ada, b_ada, norm_mix, w_in, q_norm, w_uq, kv_norm, w_ukv, hgrn_lb, hgrn_norm,
           w_out, norm_mlp, w_mlp_in, w_mlp_out, final_norm):
    B, T, D = x.shape
    L = ctx.shape[1]
    assert w_ada.shape[0] == 1, "single-layer block"
    assert T % ROW_TILE == 0 and L == ROW_TILE and T % GRID_W == 0
    assert T % MLP_ROW_TILE == 0 and T % ATTN_ROW_TILE == 0 and T % PROJ_ROW_TILE == 0
    assert PROJ_ROW_TILE % L == 0 and B % (PROJ_ROW_TILE // L) == 0 and T % L == 0 and B % HGRN_SAMPLES == 0

    pad_rows = (-(B + 1)) % 8
    cc = jnp.concatenate([c, c_ctx[None, :], jnp.zeros((pad_rows, D), F32)], axis=0)
    mod = _ada_call(cc, w_ada[0], b_ada[0][None, :])
    mod_lat = mod[:B].reshape(B, 6, D)
    mod_ctx = mod[B].reshape(1, 6, D)
    mod4 = mod_lat[:, 2:6]

    cos_t, sin_t = _rope_tables(T)
    win_p, wkv_p = _pack_w_in(w_in[0]), _pack_w_ukv(w_ukv[0])
    q, k, v, qh, lff, lfb, vi, g = _inproj_lat_call(
        x, mod_lat[:, 0:2], norm_mix, win_p, q_norm, _pack_w_uq(w_uq[0]), kv_norm, wkv_p, hgrn_lb, cos_t, sin_t)
    kc, vc, lffc, lfbc, vic = _inproj_ctx_call(
        ctx, mod_ctx[:, 0:2], norm_mix, win_p, kv_norm, wkv_p, hgrn_lb)

    tri_f, lv_f = _scan_constants(False)
    tri_b, lv_b = _scan_constants(True)
    st_f, st_b = _hgrn_ctx_call(lffc, lfbc, vic, tri_f, tri_b)
    o_f = _hgrn_call(False, (qh, lff, vi), st_f, tri_f, lv_f)
    o_hg = _hgrn_call(True, (qh, lfb, vi), st_b, tri_b, lv_b, extra=(o_f, g, hgrn_norm))

    o_mla = _attn_call(q, k, kc, v, vc)

    half = MLA_HEADS * MLA_V
    wo = w_out[0].astype(BF16)
    return _mlp_call(x, o_mla, o_hg, mod4, wo[:half], wo[half:], norm_mlp, w_mlp_in[0].astype(BF16),
                     w_mlp_out[0].astype(BF16), final_norm[None, :])
```

```python
import functools

import jax
import jax.numpy as jnp
import numpy as np
from jax import lax
from jax.experimental import pallas as pl
from jax.experimental.pallas import tpu as pltpu

F32 = jnp.float32
BF16 = jnp.bfloat16

GRID_W = 64
MLA_HEADS = 8
MLA_Q_RANK = 256
MLA_KV_RANK = 128
MLA_NOPE = 64
MLA_ROPE = 32
MLA_V = 64
HG_HEADS = 4
HG_DK = 128
HG_DV = 128
HG_KW = HG_HEADS * HG_DK
ROPE_BASE = 10000.0
EPS = 1e-6
LOG2E = 1.4426950408889634

LANES = 128
SUBLANES = 8
HEAD_SLOT = LANES
ROW_TILE = 256
HGRN_SAMPLES = 4
MLP_ROW_TILE = 512
MLP_FF_CHUNKS = 4
ADA_COL_TILE = 1024
F32_SIGN_BIT = 0x80000000
PROJ_ROW_TILE = 512
ATTN_ROW_TILE = 512
VMEM_LIMIT = 56 * 1024 * 1024

_C_CQ = (0, 256)
_C_CKV = (256, 384)
_C_KR = (384, 512)
_C_HQ = (512, 1024)
_C_HF = (1024, 1536)
_C_HB = (1536, 2048)
_C_HI = (2048, 2560)
_C_HG = (2560, 3072)
IN_P_WIDTH = 3072


def _rms(x, g):
    ms = jnp.mean(x * x, axis=-1, keepdims=True)
    return (x * lax.rsqrt(ms + EPS)) * g


def _silu(x):
    return x * jax.nn.sigmoid(x)


def _dot(a, b):
    return jnp.dot(a, b, preferred_element_type=F32)


def _dot_nt(a, b):
    return lax.dot_general(a, b, (((1,), (1,)), ((), ())), preferred_element_type=F32)


def _dot_tn(a, b):
    return lax.dot_general(a, b, (((0,), (0,)), ((), ())), preferred_element_type=F32)


def _ada_kernel(c_ref, w_ref, b_ref, o_ref):
    s = _silu(c_ref[...])
    o_ref[...] = jnp.dot(s, w_ref[...], preferred_element_type=F32,
                         precision=lax.Precision.HIGHEST) + b_ref[...]


def _ada_call(cc, w_ada, b_ada):
    rows, d = cc.shape
    n = w_ada.shape[1]
    tn = ADA_COL_TILE
    return pl.pallas_call(
        _ada_kernel,
        grid=(n // tn,),
        in_specs=[
            pl.BlockSpec((rows, d), lambda j: (0, 0)),
            pl.BlockSpec((d, tn), lambda j: (0, j)),
            pl.BlockSpec((1, tn), lambda j: (0, j)),
        ],
        out_specs=pl.BlockSpec((rows, tn), lambda j: (0, j)),
        out_shape=jax.ShapeDtypeStruct((rows, n), F32),
        compiler_params=pltpu.CompilerParams(dimension_semantics=("arbitrary",), vmem_limit_bytes=VMEM_LIMIT),
        name="ada_mod",
    )(cc, w_ada, b_ada)


def _lane_lt(n_rows, bound):
    return lax.broadcasted_iota(jnp.int32, (n_rows, LANES), 1) < bound


def _rotate_half_lanes(x):
    lane = lax.broadcasted_iota(jnp.int32, x.shape, 1)
    n = MLA_ROPE // 4
    return jnp.where((lane & (2 * n - 1)) < n, -pltpu.roll(x, LANES - n, 1), pltpu.roll(x, n, 1))


def _head_slots(nope, rope_lo, rope_hi):
    first = _lane_lt(nope.shape[0], MLA_NOPE)
    return (jnp.where(first, nope, rope_lo), jnp.where(first, pltpu.roll(nope, MLA_NOPE, 1), rope_hi))


def _inproj_kernel(is_ctx, *refs):
    if is_ctx:
        (x_ref, mod_ref, nmix_ref, win_ref, kvn_ref, wkv_ref, lb_ref, trif_ref, trib_ref,
         k_out, v_out, stf_out, stb_out) = refs
    else:
        (x_ref, mod_ref, nmix_ref, win_ref, qn_ref, wq_ref, kvn_ref, wkv_ref, lb_ref, cos_ref, sin_ref,
         q_out, k_out, v_out, qh_out, lff_out, lfb_out, vi_out, g_out) = refs
    mod = mod_ref[0]
    h = _rms(x_ref[0], nmix_ref[...]) * (1.0 + mod[1:2]) + mod[0:1]
    hb = h.astype(BF16)
    n_pairs = MLA_HEADS // 2
    half = n_pairs * LANES

    def proj(c):
        return _dot(hb, win_ref[:, c[0]:c[1]])

    def put(ref, val):
        ref[...] = val.astype(ref.dtype).reshape(ref.shape)

    def pair(a, j):
        return a[:, j * LANES:(j + 1) * LANES]

    lbr = lb_ref[...]
    e = jnp.exp(lbr - jnp.max(lbr, axis=0, keepdims=True))
    lb = e[0] / jnp.sum(e, axis=0)
    lbf = lb[0:1]
    lbb = lb[1:2]

    ckv = proj(_C_CKV)
    kr = proj(_C_KR)
    if not is_ctx:
        cq = proj(_C_CQ)
    lf_f = jnp.log(lbf + (1.0 - lbf) * jax.nn.sigmoid(proj(_C_HF)))
    vi = proj(_C_HI).astype(BF16)
    kv = _dot(_rms(ckv, kvn_ref[...]).astype(BF16), wkv_ref[...])
    lf_b = jnp.log(lbb + (1.0 - lbb) * jax.nn.sigmoid(proj(_C_HB)))
    if is_ctx:
        rows = x_ref.shape[1] // stf_out.shape[0]
        for s in range(stf_out.shape[0]):
            rs = slice(s * rows, (s + 1) * rows)
            for reverse, lf, tri_ref, st_out in ((False, lf_f, trif_ref, stf_out), (True, lf_b, trib_ref, stb_out)):
                _hgrn_chunk_state(reverse, lf[rs], vi[rs], tri_ref[...], st_out, s)
    else:
        put(lff_out, lf_f)
        put(vi_out, vi)
        put(lfb_out, lf_b)
    if not is_ctx:
        cos = cos_ref[...]
        sin = sin_ref[...]
        kr = kr * cos + _rotate_half_lanes(kr) * sin
        qq = _dot(_rms(cq, qn_ref[...]).astype(BF16), wq_ref[...])
    if not is_ctx:
        put(qh_out, _silu(proj(_C_HQ)))
    k_slots = []
    for j in range(n_pairs):
        k_slots += _head_slots(pair(kv, j), kr, kr)
    put(k_out, jnp.concatenate(k_slots, axis=1))
    lane = lax.broadcasted_iota(jnp.int32, (kv.shape[0], LANES), 1)
    ones_lane = jnp.where(lane == MLA_V, 1.0, 0.0)
    v_slots = []
    for j in range(n_pairs):
        v_slots += _head_slots(pair(kv, n_pairs + j), ones_lane, ones_lane)
    put(v_out, jnp.concatenate(v_slots, axis=1))
    if not is_ctx:
        scale = (MLA_NOPE + MLA_ROPE) ** -0.5 * LOG2E
        q_slots = []
        for j in range(n_pairs):
            r = pair(qq, n_pairs + j)
            r = (r * cos + _rotate_half_lanes(r) * sin) * scale
            q_slots += _head_slots(pair(qq, j) * scale, r, pltpu.roll(r, LANES - MLA_ROPE, 1))
        put(q_out, jnp.concatenate(q_slots, axis=1))
        put(g_out, _silu(proj(_C_HG)))


def _inproj_lat_shapes(B, rows):
    nq = MLA_HEADS * HEAD_SLOT
    return [
        ((B, rows, nq), BF16),
        ((B, rows, nq), BF16),
        ((B, rows, nq), BF16),
        ((B, rows, HG_KW), BF16),
        ((B, rows, HG_KW), F32),
        ((B, rows, HG_KW), F32),
        ((B, rows, HG_KW), BF16),
        ((B, rows, HG_KW), BF16),
    ]


def _inproj_lat_call(x, mod_lat, nmix, win_p, qn, wq_p, kvn, wkv_p, lb_raw, cos_t, sin_t):
    B, T, D = x.shape
    R = PROJ_ROW_TILE
    nq = MLA_HEADS * HEAD_SLOT
    row_idx = lambda b, i: (b, i, 0)
    const2 = lambda b, i: (0, 0)
    in_specs = [
        pl.BlockSpec((1, R, D), row_idx),
        pl.BlockSpec((1, 2, D), lambda b, i: (b, 0, 0)),
        pl.BlockSpec((1, D), const2),
        pl.BlockSpec((D, IN_P_WIDTH), const2, pipeline_mode=pl.Buffered(1)),
        pl.BlockSpec((1, MLA_Q_RANK), const2),
        pl.BlockSpec((MLA_Q_RANK, nq), const2, pipeline_mode=pl.Buffered(1)),
        pl.BlockSpec((1, MLA_KV_RANK), const2),
        pl.BlockSpec((MLA_KV_RANK, nq), const2, pipeline_mode=pl.Buffered(1)),
        pl.BlockSpec(lb_raw.shape, lambda b, i: (0, 0, 0)),
        pl.BlockSpec((R, LANES), lambda b, i: (i, 0)),
        pl.BlockSpec((R, LANES), lambda b, i: (i, 0)),
    ]
    shapes = _inproj_lat_shapes(B, T)
    return pl.pallas_call(
        functools.partial(_inproj_kernel, False),
        grid=(B, T // R),
        in_specs=in_specs,
        out_specs=[pl.BlockSpec((1, R, s[2]), row_idx) for s, _ in shapes],
        out_shape=[jax.ShapeDtypeStruct(s, d) for s, d in shapes],
        compiler_params=pltpu.CompilerParams(dimension_semantics=("arbitrary", "arbitrary"),
                                             vmem_limit_bytes=VMEM_LIMIT),
        name="inproj_lat",
    )(x, mod_lat, nmix, win_p, qn, wq_p, kvn, wkv_p, lb_raw, cos_t, sin_t)


def _inproj_ctx_call(ctx, mod_ctx, nmix, win_p, kvn, wkv_p, lb_raw, tri_f, tri_b):
    B, L, D = ctx.shape
    R = PROJ_ROW_TILE
    per = R // L
    nq = MLA_HEADS * HEAD_SLOT
    x2 = ctx.reshape(B // per, R, D)
    const2 = lambda i: (0, 0)
    in_specs = [
        pl.BlockSpec((1, R, D), lambda i: (i, 0, 0)),
        pl.BlockSpec((1, 2, D), lambda i: (0, 0, 0)),
        pl.BlockSpec((1, D), const2),
        pl.BlockSpec((D, IN_P_WIDTH), const2, pipeline_mode=pl.Buffered(1)),
        pl.BlockSpec((1, MLA_KV_RANK), const2),
        pl.BlockSpec((MLA_KV_RANK, nq), const2, pipeline_mode=pl.Buffered(1)),
        pl.BlockSpec(lb_raw.shape, lambda i: (0, 0, 0)),
        pl.BlockSpec((L, L), const2),
        pl.BlockSpec((L, L), const2),
    ]
    kv_spec = pl.BlockSpec((per, L, nq), lambda i: (i, 0, 0))
    st_spec = pl.BlockSpec((per, HG_HEADS, HG_DV, HG_DK), lambda i: (i, 0, 0, 0))
    kv_shape = jax.ShapeDtypeStruct((B, L, nq), BF16)
    st_shape = jax.ShapeDtypeStruct((B, HG_HEADS, HG_DV, HG_DK), F32)
    return pl.pallas_call(
        functools.partial(_inproj_kernel, True),
        grid=(B // per,),
        in_specs=in_specs,
        out_specs=[kv_spec, kv_spec, st_spec, st_spec],
        out_shape=[kv_shape, kv_shape, st_shape, st_shape],
        compiler_params=pltpu.CompilerParams(dimension_semantics=("arbitrary",), vmem_limit_bytes=VMEM_LIMIT),
        name="inproj_ctx",
    )(x2, mod_ctx, nmix, win_p, kvn, wkv_p, lb_raw, tri_f, tri_b)


N_LEVELS = 8
N_SMALL_LEVELS = 2


def _neg_abs(x):
    bits = lax.bitcast_convert_type(x, jnp.uint32) | jnp.uint32(F32_SIGN_BIT)
    return lax.bitcast_convert_type(bits, F32)


def _replace_bit(y, row, bit, value):
    step = 1 << bit
    n = y.shape[0]
    has = ((row >> bit) & 1) == 1
    if value == 1:
        return jnp.where(has, y, pltpu.roll(y, n - step, 0))
    return jnp.where(has, pltpu.roll(y, step, 0), y)


def _cum_log2_decay(lf, tri):
    l_hi = lf.astype(BF16)
    r1 = lf - l_hi.astype(F32)
    l_mid = r1.astype(BF16)
    l_lo = (r1 - l_mid.astype(F32)).astype(BF16)
    return (_dot(tri, l_hi) + _dot(tri, l_mid) + _dot(tri, l_lo)) * LOG2E


def _hgrn_chunk_state(reverse, lf, vi, tri, st_out, n):
    b = _cum_log2_decay(lf, tri)
    end_row = 0 if reverse else lf.shape[0] - 1
    k_end = ((1.0 - jnp.exp(lf)) * jnp.exp2(b[end_row:end_row + 1] - b)).astype(BF16)
    for h in range(HG_HEADS):
        sl = slice(h * HG_DK, (h + 1) * HG_DK)
        st_out[n, h] = _dot_tn(vi[:, sl], k_end[:, sl])


def _hgrn_kernel(reverse, *refs):
    st0_ref, st_ref = refs[3], refs[-1]

    @pl.when(pl.program_id(1) == 0)
    def _():
        st_ref[...] = st0_ref[...].reshape(st_ref.shape)

    stages = [_hgrn_sample(reverse, n, *refs) for n in range(refs[0].shape[0])]
    while stages:
        stages = [s for s in stages if next(s, False)]


def _hgrn_sample(reverse, n, qh_ref, lf_ref, vi_ref, st0_ref, tri_ref, lv_ref, *rest):
    if reverse:
        of_ref, g_ref, hgn_ref, out_ref, st_ref = rest
    else:
        out_ref, st_ref = rest
    lf = lf_ref[n]
    vi_in = vi_ref[n]
    b_all = _cum_log2_decay(lf, tri_ref[...])
    f_all = jnp.exp(lf)
    k_all = 1.0 - f_all
    q_all = qh_ref[n].astype(F32)
    lv = lv_ref[...]
    R = lf.shape[0]
    H = R // 2
    row = lax.broadcasted_iota(jnp.int32, (R, LANES), 0)
    near, far = (0, 1) if not reverse else (1, 0)
    end_row = 0 if reverse else R - 1
    heads = range(HG_HEADS)
    sls = [slice(h * HG_DK, (h + 1) * HG_DK) for h in heads]
    bs = [b_all[:, sl] for sl in sls]
    qs = [q_all[:, sl] for sl in sls]
    ks = [k_all[:, sl] for sl in sls]
    fs = [f_all[:, sl] for sl in sls]
    vs = [vi_in[:, sl] for sl in sls]

    diag = lv == N_LEVELS - 1
    a0 = [jnp.where(diag, _dot_nt(qs[h][:H].astype(BF16), ks[h][:H].astype(BF16)), 0.0) for h in heads]
    a1 = [jnp.where(diag, _dot_nt(qs[h][H:].astype(BF16), ks[h][H:].astype(BF16)), 0.0) for h in heads]
    a_off = [None] * HG_HEADS
    zs = list(bs)
    for lev in range(N_LEVELS):
        m = 1 << lev
        mask = lv == lev
        q_side = ((row >> lev) & 1) == far
        for h in heads:
            bh, qh, kh = bs[h], qs[h], ks[h]
            if lev == 0:
                zs[h] = _replace_bit(zs[h], row, lev, far)
                u = jnp.where(q_side, qh * fs[h], kh).astype(BF16)
            elif lev < N_SMALL_LEVELS:
                bmid = _replace_bit(zs[h], row, lev, near)
                zs[h] = _replace_bit(zs[h], row, lev, far)
                qk = jnp.where(q_side, qh, kh)
                arg = _neg_abs(bh - bmid)
            elif 2 * m == SUBLANES:
                ref = m if reverse else m - 1
                bmid = jnp.concatenate([jnp.broadcast_to(bh[lo + ref:lo + ref + 1], (SUBLANES, HG_DK))
                                        for lo in range(0, R, SUBLANES)], axis=0)
                qk = jnp.where(q_side, qh, kh)
                arg = _neg_abs(bh - bmid)
            else:
                args, parts = [], []
                for j in range(R // (2 * m)):
                    lo = j * 2 * m
                    ref = lo + (m if reverse else m - 1)
                    bref = bh[ref:ref + 1]
                    if reverse:
                        args += [bh[lo:lo + m] - bref, bref - bh[lo + m:lo + 2 * m]]
                        parts += [qh[lo:lo + m], kh[lo + m:lo + 2 * m]]
                    else:
                        args += [bref - bh[lo:lo + m], bh[lo + m:lo + 2 * m] - bref]
                        parts += [kh[lo:lo + m], qh[lo + m:lo + 2 * m]]
                arg = jnp.concatenate(args, axis=0)
                qk = jnp.concatenate(parts, axis=0)
            if lev > 0:
                u = (qk * jnp.exp2(arg)).astype(BF16)
            if lev < N_LEVELS - 1:
                a0[h] = jnp.where(mask, _dot_nt(u[:H], u[:H]), a0[h])
                a1[h] = jnp.where(mask, _dot_nt(u[H:], u[H:]), a1[h])
            elif reverse:
                a_off[h] = _dot_nt(u[:H], u[H:])
            else:
                a_off[h] = _dot_nt(u[H:], u[:H])
        yield True

    for h in heads:
        sl, bh, qh, kh, vh = sls[h], bs[h], qs[h], ks[h], vs[h]
        zend = bh[end_row:end_row + 1]
        st = st_ref[n * HG_HEADS + h]
        if reverse:
            o_in = jnp.concatenate([
                _dot(jnp.concatenate([a0[h], a_off[h]], axis=1).astype(BF16), vh),
                _dot(a1[h].astype(BF16), vh[H:])], axis=0)
        else:
            o_in = jnp.concatenate([
                _dot(a0[h].astype(BF16), vh[:H]),
                _dot(jnp.concatenate([a_off[h], a1[h]], axis=1).astype(BF16), vh)], axis=0)
        o = o_in + _dot_nt((qh * jnp.exp2(bh)).astype(BF16), st.astype(BF16))
        k_end = (kh * jnp.exp2(zend - bh)).astype(BF16)
        st_ref[n * HG_HEADS + h] = jnp.exp2(zend) * st + _dot_tn(vh, k_end)

        if reverse:
            tot = of_ref[n, :, sl] + o
            out_ref[n, :, sl] = (_rms(tot, hgn_ref[...]) * g_ref[n, :, sl].astype(F32)).astype(out_ref.dtype)
        else:
            out_ref[n, :, sl] = o
        yield True


def _hgrn_call(reverse, lat, st0, tri, lv, extra=()):
    B, T, KW = lat[0].shape
    R = ROW_TILE
    n_lat = T // R

    if reverse:
        def lat_idx(b, i):
            return (b, n_lat - 1 - i, 0)
    else:
        def lat_idx(b, i):
            return (b, i, 0)

    NS = HGRN_SAMPLES
    const2 = lambda b, i: (0, 0)
    in_specs = [pl.BlockSpec((NS, R, KW), lat_idx)] * 3 + [
        pl.BlockSpec((NS, HG_HEADS, HG_DV, HG_DK), lambda b, i: (b, 0, 0, 0)),
        pl.BlockSpec((R, R), const2),
        pl.BlockSpec((R // 2, R // 2), const2),
    ]
    if reverse:
        in_specs += [
            pl.BlockSpec((NS, R, KW), lat_idx),
            pl.BlockSpec((NS, R, KW), lat_idx),
            pl.BlockSpec((1, HG_DV), const2),
        ]
        out_dtype = BF16
    else:
        out_dtype = F32
    return pl.pallas_call(
        functools.partial(_hgrn_kernel, reverse),
        grid=(B // NS, n_lat),
        in_specs=in_specs,
        out_specs=pl.BlockSpec((NS, R, KW), lat_idx),
        out_shape=jax.ShapeDtypeStruct((B, T, KW), out_dtype),
        scratch_shapes=[pltpu.VMEM((NS * HG_HEADS, HG_DV, HG_DK), F32)],
        compiler_params=pltpu.CompilerParams(dimension_semantics=("arbitrary", "arbitrary"),
                                             vmem_limit_bytes=VMEM_LIMIT),
        name="hgrn_bwd" if reverse else "hgrn_fwd",
    )(*lat, st0, tri, lv, *extra)


def _attn_kernel(q_ref, k_ref, kc_ref, v_ref, vc_ref, o_ref):
    n_lat = k_ref.shape[1]

    def scores(h):
        sl = slice(h * HEAD_SLOT, (h + 1) * HEAD_SLOT)
        qh = q_ref[0, :, sl]
        return jnp.concatenate([_dot_nt(qh, k_ref[0, :, sl]), _dot_nt(qh, kc_ref[0, :, sl])], axis=1)

    def attend(h, s):
        sl = slice(h * HEAD_SLOT, (h + 1) * HEAD_SLOT)
        pb = jnp.exp2(s - jnp.max(s, axis=-1, keepdims=True)).astype(BF16)
        o = _dot(pb[:, :n_lat], v_ref[0, :, sl]) + _dot(pb[:, n_lat:], vc_ref[0, :, sl])
        return o * (1.0 / o[:, MLA_V:MLA_V + 1])

    outs = []
    s_next = scores(0)
    for h in range(MLA_HEADS):
        s = s_next
        if h + 1 < MLA_HEADS:
            s_next = scores(h + 1)
        outs.append(attend(h, s))
    first = _lane_lt(q_ref.shape[1], MLA_V)
    pairs = [jnp.where(first, outs[2 * j], pltpu.roll(outs[2 * j + 1], MLA_V, 1)) for j in range(MLA_HEADS // 2)]
    o_ref[0] = jnp.concatenate(pairs, axis=-1).astype(o_ref.dtype)


def _attn_call(q, k, k_ctx, v, v_ctx):
    B, T, nq = q.shape
    R = ATTN_ROW_TILE
    whole = lambda a: pl.BlockSpec((1,) + a.shape[1:], lambda b, i: (b, 0, 0))
    return pl.pallas_call(
        _attn_kernel,
        grid=(B, T // R),
        in_specs=[pl.BlockSpec((1, R, nq), lambda b, i: (b, i, 0)), whole(k), whole(k_ctx), whole(v), whole(v_ctx)],
        out_specs=pl.BlockSpec((1, R, MLA_HEADS * MLA_V), lambda b, i: (b, i, 0)),
        out_shape=jax.ShapeDtypeStruct((B, T, MLA_HEADS * MLA_V), BF16),
        compiler_params=pltpu.CompilerParams(dimension_semantics=("arbitrary", "arbitrary"),
                                             vmem_limit_bytes=VMEM_LIMIT),
        name="mla_attn",
    )(q, k, k_ctx, v, v_ctx)


def _mlp_kernel(n_ff_chunks, x_ref, om_ref, oh_ref, mod_ref, woa_ref, wob_ref, nmlp_ref, w1_ref, w2_ref,
                fn_ref, o_ref):
    mod = mod_ref[0]
    y = _dot(om_ref[0], woa_ref[...]) + _dot(oh_ref[0], wob_ref[...])
    x1 = x_ref[0] + mod[0:1] * y
    h2 = (_rms(x1, nmlp_ref[...]) * (1.0 + mod[2:3]) + mod[1:2]).astype(BF16)
    ff = w1_ref.shape[1] // n_ff_chunks
    acc = jnp.zeros(x1.shape, F32)
    for c in range(n_ff_chunks):
        u = jnp.maximum(_dot(h2, w1_ref[:, c * ff:(c + 1) * ff]), 0.0)
        acc = acc + _dot((u * u).astype(BF16), w2_ref[c * ff:(c + 1) * ff, :])
    x2 = x1 + mod[3:4] * acc
    o_ref[0] = _rms(x2, fn_ref[...])


def _mlp_call(x, o_mla, o_hg, mod4, woa, wob, nmlp, w1, w2, fnorm):
    B, T, D = x.shape
    R = MLP_ROW_TILE
    dff = w1.shape[1]
    half = o_mla.shape[2]
    row_idx = lambda b, i: (b, i, 0)
    const2 = lambda b, i: (0, 0)
    return pl.pallas_call(
        functools.partial(_mlp_kernel, MLP_FF_CHUNKS),
        grid=(B, T // R),
        in_specs=[
            pl.BlockSpec((1, R, D), row_idx),
            pl.BlockSpec((1, R, half), row_idx),
            pl.BlockSpec((1, R, half), row_idx),
            pl.BlockSpec((1, 4, D), lambda b, i: (b, 0, 0)),
            pl.BlockSpec((half, D), const2, pipeline_mode=pl.Buffered(1)),
            pl.BlockSpec((half, D), const2, pipeline_mode=pl.Buffered(1)),
            pl.BlockSpec((1, D), const2),
            pl.BlockSpec((D, dff), const2, pipeline_mode=pl.Buffered(1)),
            pl.BlockSpec((dff, D), const2, pipeline_mode=pl.Buffered(1)),
            pl.BlockSpec((1, D), const2),
        ],
        out_specs=pl.BlockSpec((1, R, D), row_idx),
        out_shape=jax.ShapeDtypeStruct((B, T, D), F32),
        compiler_params=pltpu.CompilerParams(dimension_semantics=("arbitrary", "arbitrary"),
                                             vmem_limit_bytes=VMEM_LIMIT),
        name="out_mlp",
    )(x, o_mla, o_hg, mod4, woa, wob, nmlp, w1, w2, fnorm)


def _rope_slot(w):
    z = jnp.zeros(w.shape[:-1] + (MLA_NOPE,), w.dtype)
    z2 = jnp.zeros(w.shape[:-1] + (HEAD_SLOT - MLA_NOPE - MLA_ROPE,), w.dtype)
    return jnp.concatenate([z, w, z2], axis=-1)


def _pack_w_in(w_in):
    w = w_in.astype(BF16)
    n_mla = MLA_Q_RANK + MLA_KV_RANK
    return jnp.concatenate([w[:, :n_mla], _rope_slot(w[:, n_mla:n_mla + MLA_ROPE]), w[:, n_mla + MLA_ROPE:]], axis=1)


def _pack_w_uq(w_uq):
    k = w_uq.shape[0]
    w3 = w_uq.reshape(k, MLA_HEADS, MLA_NOPE + MLA_ROPE)
    nope = w3[..., :MLA_NOPE].reshape(k, MLA_HEADS * MLA_NOPE)
    rope = w3[..., MLA_NOPE:].reshape(k, MLA_HEADS // 2, 2 * MLA_ROPE)
    rope = jnp.concatenate([jnp.zeros((k, MLA_HEADS // 2, LANES - 2 * MLA_ROPE), w_uq.dtype), rope], axis=-1)
    return jnp.concatenate([nope, rope.reshape(k, (MLA_HEADS // 2) * LANES)], axis=1).astype(BF16)


def _pack_w_ukv(w_ukv):
    k = w_ukv.shape[0]
    w3 = w_ukv.reshape(k, MLA_HEADS, MLA_NOPE + MLA_V)
    kk = w3[..., :MLA_NOPE].reshape(k, MLA_HEADS * MLA_NOPE)
    vv = w3[..., MLA_NOPE:].reshape(k, MLA_HEADS * MLA_V)
    return jnp.concatenate([kk, vv], axis=1).astype(BF16)


def _rope_tables(T):
    rows = T // GRID_W
    row = np.repeat(np.arange(rows), GRID_W).astype(np.float64)
    col = np.tile(np.arange(GRID_W), rows).astype(np.float64)
    n = MLA_ROPE // 4
    inv = ROPE_BASE ** (-np.arange(n, dtype=np.float64) / n)
    ar, ac = row[:, None] * inv, col[:, None] * inv
    cos32 = np.concatenate([np.cos(ar), np.cos(ar), np.cos(ac), np.cos(ac)], axis=-1)
    sin32 = np.concatenate([np.sin(ar), np.sin(ar), np.sin(ac), np.sin(ac)], axis=-1)
    cos_t = np.concatenate([np.ones((T, MLA_NOPE)), cos32, cos32], axis=-1)
    sin_t = np.concatenate([np.zeros((T, MLA_NOPE)), sin32, sin32], axis=-1)
    return jnp.asarray(cos_t, F32), jnp.asarray(sin_t, F32)


def _scan_constants(reverse):
    R = ROW_TILE
    t = np.arange(R, dtype=np.int32)[:, None]
    s = np.arange(R, dtype=np.int32)[None, :]
    if reverse:
        t, s = s, t
    tri = (t >= s)
    H = R // 2
    t, s = np.broadcast_to(t, (R, R))[:H, :H], np.broadcast_to(s, (R, R))[:H, :H]
    x = t ^ s
    lev = np.zeros((H, H), np.int32)
    for bit in range(1, N_LEVELS - 1):
        lev = np.where((x >> bit) > 0, bit, lev)
    lv = np.where(t > s, lev, np.where(t == s, N_LEVELS - 1, -1))
    return jnp.asarray(tri, BF16), jnp.asarray(lv, jnp.int32)


def kernel(x, c, ctx, c_ctx, w_ada, b_ada, norm_mix, w_in, q_norm, w_uq, kv_norm, w_ukv, hgrn_lb, hgrn_norm,
           w_out, norm_mlp, w_mlp_in, w_mlp_out, final_norm):
    B, T, D = x.shape
    L = ctx.shape[1]
    assert w_ada.shape[0] == 1, "single-layer block"
    assert T % ROW_TILE == 0 and L == ROW_TILE and T % GRID_W == 0
    assert T % MLP_ROW_TILE == 0 and T % ATTN_ROW_TILE == 0 and T % PROJ_ROW_TILE == 0
    assert PROJ_ROW_TILE % L == 0 and B % (PROJ_ROW_TILE // L) == 0 and T % L == 0 and B % HGRN_SAMPLES == 0

    pad_rows = (-(B + 1)) % 8
    cc = jnp.concatenate([c, c_ctx[None, :], jnp.zeros((pad_rows, D), F32)], axis=0)
    mod = _ada_call(cc, w_ada[0], b_ada[0][None, :])
    mod_lat = mod[:B].reshape(B, 6, D)
    mod_ctx = mod[B].reshape(1, 6, D)
    mod4 = mod_lat[:, 2:6]

    cos_t, sin_t = _rope_tables(T)
    win_p, wkv_p = _pack_w_in(w_in[0]), _pack_w_ukv(w_ukv[0])
    q, k, v, qh, lff, lfb, vi, g = _inproj_lat_call(
        x, mod_lat[:, 0:2], norm_mix, win_p, q_norm, _pack_w_uq(w_uq[0]), kv_norm, wkv_p, hgrn_lb, cos_t, sin_t)
    tri_f, lv_f = _scan_constants(False)
    tri_b, lv_b = _scan_constants(True)
    kc, vc, st_f, st_b = _inproj_ctx_call(
        ctx, mod_ctx[:, 0:2], norm_mix, win_p, kv_norm, wkv_p, hgrn_lb, tri_f, tri_b)

    o_f = _hgrn_call(False, (qh, lff, vi), st_f, tri_f, lv_f)
    o_hg = _hgrn_call(True, (qh, lfb, vi), st_b, tri_b, lv_b, extra=(o_f, g, hgrn_norm))

    o_mla = _attn_call(q, k, kc, v, vc)

    half = MLA_HEADS * MLA_V
    wo = w_out[0].astype(BF16)
    return _mlp_call(x, o_mla, o_hg, mod4, wo[:half], wo[half:], norm_mlp, w_mlp_in[0].astype(BF16),
                     w_mlp_out[0].astype(BF16), final_norm[None, :])
```

```python
import functools

import jax
import jax.numpy as jnp
import numpy as np
from jax import lax
from jax.experimental import pallas as pl
from jax.experimental.pallas import tpu as pltpu

F32 = jnp.float32
BF16 = jnp.bfloat16

GRID_W = 64
MLA_HEADS = 8
MLA_Q_RANK = 256
MLA_KV_RANK = 128
MLA_NOPE = 64
MLA_ROPE = 32
MLA_V = 64
HG_HEADS = 4
HG_DK = 128
HG_DV = 128
HG_KW = HG_HEADS * HG_DK
ROPE_BASE = 10000.0
EPS = 1e-6
LOG2E = 1.4426950408889634

LANES = 128
SUBLANES = 8
HEAD_SLOT = LANES
ROW_TILE = 256
HGRN_SAMPLES = 4
MLP_ROW_TILE = 1024
MLP_FF_CHUNKS = 4
ADA_COL_TILE = 1024
F32_SIGN_BIT = 0x80000000
PROJ_ROW_TILE = 1024
ATTN_ROW_TILE = 512
VMEM_LIMIT = 56 * 1024 * 1024

_C_CQ = (0, 256)
_C_CKV = (256, 384)
_C_KR = (384, 512)
_C_HQ = (512, 1024)
_C_HF = (1024, 1536)
_C_HB = (1536, 2048)
_C_HI = (2048, 2560)
_C_HG = (2560, 3072)
IN_P_WIDTH = 3072


def _rms(x, g):
    ms = jnp.mean(x * x, axis=-1, keepdims=True)
    return (x * lax.rsqrt(ms + EPS)) * g


def _silu(x):
    return x * jax.nn.sigmoid(x)


def _dot(a, b):
    return jnp.dot(a, b, preferred_element_type=F32)


def _dot_nt(a, b):
    return lax.dot_general(a, b, (((1,), (1,)), ((), ())), preferred_element_type=F32)


def _dot_tn(a, b):
    return lax.dot_general(a, b, (((0,), (0,)), ((), ())), preferred_element_type=F32)


def _ada_kernel(c_ref, w_ref, b_ref, o_ref):
    s = _silu(c_ref[...])
    o_ref[...] = jnp.dot(s, w_ref[...], preferred_element_type=F32,
                         precision=lax.Precision.HIGHEST) + b_ref[...]


def _ada_call(cc, w_ada, b_ada):
    rows, d = cc.shape
    n = w_ada.shape[1]
    tn = ADA_COL_TILE
    return pl.pallas_call(
        _ada_kernel,
        grid=(n // tn,),
        in_specs=[
            pl.BlockSpec((rows, d), lambda j: (0, 0)),
            pl.BlockSpec((d, tn), lambda j: (0, j)),
            pl.BlockSpec((1, tn), lambda j: (0, j)),
        ],
        out_specs=pl.BlockSpec((rows, tn), lambda j: (0, j)),
        out_shape=jax.ShapeDtypeStruct((rows, n), F32),
        compiler_params=pltpu.CompilerParams(dimension_semantics=("arbitrary",), vmem_limit_bytes=VMEM_LIMIT),
        name="ada_mod",
    )(cc, w_ada, b_ada)


def _lane_lt(n_rows, bound):
    return lax.broadcasted_iota(jnp.int32, (n_rows, LANES), 1) < bound


def _rotate_half_lanes(x):
    lane = lax.broadcasted_iota(jnp.int32, x.shape, 1)
    n = MLA_ROPE // 4
    return jnp.where((lane & (2 * n - 1)) < n, -pltpu.roll(x, LANES - n, 1), pltpu.roll(x, n, 1))


def _head_slots(nope, rope_lo, rope_hi):
    first = _lane_lt(nope.shape[0], MLA_NOPE)
    return (jnp.where(first, nope, rope_lo), jnp.where(first, pltpu.roll(nope, MLA_NOPE, 1), rope_hi))


def _inproj_kernel(is_ctx, *refs):
    if is_ctx:
        (x_ref, mod_ref, nmix_ref, win_ref, kvn_ref, wkv_ref, lb_ref, trif_ref, trib_ref,
         k_out, v_out, stf_out, stb_out) = refs
    else:
        (x_ref, mod_ref, nmix_ref, win_ref, qn_ref, wq_ref, kvn_ref, wkv_ref, lb_ref, cos_ref, sin_ref,
         q_out, k_out, v_out, qh_out, lff_out, lfb_out, vi_out, g_out) = refs
    mod = mod_ref[0]
    h = _rms(x_ref[0], nmix_ref[...]) * (1.0 + mod[1:2]) + mod[0:1]
    hb = h.astype(BF16)
    n_pairs = MLA_HEADS // 2
    half = n_pairs * LANES

    def proj(c):
        return _dot(hb, win_ref[:, c[0]:c[1]])

    def put(ref, val):
        ref[...] = val.astype(ref.dtype).reshape(ref.shape)

    def pair(a, j):
        return a[:, j * LANES:(j + 1) * LANES]

    lbr = lb_ref[...]
    e = jnp.exp(lbr - jnp.max(lbr, axis=0, keepdims=True))
    lb = e[0] / jnp.sum(e, axis=0)
    lbf = lb[0:1]
    lbb = lb[1:2]

    ckv = proj(_C_CKV)
    kr = proj(_C_KR)
    if not is_ctx:
        cq = proj(_C_CQ)
    lf_f = jnp.log(lbf + (1.0 - lbf) * jax.nn.sigmoid(proj(_C_HF)))
    vi = proj(_C_HI).astype(BF16)
    kv = _dot(_rms(ckv, kvn_ref[...]).astype(BF16), wkv_ref[...])
    lf_b = jnp.log(lbb + (1.0 - lbb) * jax.nn.sigmoid(proj(_C_HB)))
    if is_ctx:
        rows = x_ref.shape[1] // stf_out.shape[0]
        for s in range(stf_out.shape[0]):
            rs = slice(s * rows, (s + 1) * rows)
            for reverse, lf, tri_ref, st_out in ((False, lf_f, trif_ref, stf_out), (True, lf_b, trib_ref, stb_out)):
                _hgrn_chunk_state(reverse, lf[rs], vi[rs], tri_ref[...], st_out, s)
    else:
        put(lff_out, lf_f)
        put(vi_out, vi)
        put(lfb_out, lf_b)
    if not is_ctx:
        cos = cos_ref[...]
        sin = sin_ref[...]
        kr = kr * cos + _rotate_half_lanes(kr) * sin
        qq = _dot(_rms(cq, qn_ref[...]).astype(BF16), wq_ref[...])
    if not is_ctx:
        put(qh_out, _silu(proj(_C_HQ)))
    k_slots = []
    for j in range(n_pairs):
        k_slots += _head_slots(pair(kv, j), kr, kr)
    put(k_out, jnp.concatenate(k_slots, axis=1))
    lane = lax.broadcasted_iota(jnp.int32, (kv.shape[0], LANES), 1)
    ones_lane = jnp.where(lane == MLA_V, 1.0, 0.0)
    v_slots = []
    for j in range(n_pairs):
        v_slots += _head_slots(pair(kv, n_pairs + j), ones_lane, ones_lane)
    put(v_out, jnp.concatenate(v_slots, axis=1))
    if not is_ctx:
        scale = (MLA_NOPE + MLA_ROPE) ** -0.5 * LOG2E
        q_slots = []
        for j in range(n_pairs):
            r = pair(qq, n_pairs + j)
            r = (r * cos + _rotate_half_lanes(r) * sin) * scale
            q_slots += _head_slots(pair(qq, j) * scale, r, pltpu.roll(r, LANES - MLA_ROPE, 1))
        put(q_out, jnp.concatenate(q_slots, axis=1))
        put(g_out, _silu(proj(_C_HG)))


def _inproj_lat_shapes(B, rows):
    nq = MLA_HEADS * HEAD_SLOT
    return [
        ((B, rows, nq), BF16),
        ((B, rows, nq), BF16),
        ((B, rows, nq), BF16),
        ((B, rows, HG_KW), BF16),
        ((B, rows, HG_KW), F32),
        ((B, rows, HG_KW), F32),
        ((B, rows, HG_KW), BF16),
        ((B, rows, HG_KW), BF16),
    ]


def _inproj_lat_call(x, mod_lat, nmix, win_p, qn, wq_p, kvn, wkv_p, lb_raw, cos_t, sin_t):
    B, T, D = x.shape
    R = PROJ_ROW_TILE
    nq = MLA_HEADS * HEAD_SLOT
    row_idx = lambda b, i: (b, i, 0)
    const2 = lambda b, i: (0, 0)
    in_specs = [
        pl.BlockSpec((1, R, D), row_idx),
        pl.BlockSpec((1, 2, D), lambda b, i: (b, 0, 0)),
        pl.BlockSpec((1, D), const2),
        pl.BlockSpec((D, IN_P_WIDTH), const2, pipeline_mode=pl.Buffered(1)),
        pl.BlockSpec((1, MLA_Q_RANK), const2),
        pl.BlockSpec((MLA_Q_RANK, nq), const2, pipeline_mode=pl.Buffered(1)),
        pl.BlockSpec((1, MLA_KV_RANK), const2),
        pl.BlockSpec((MLA_KV_RANK, nq), const2, pipeline_mode=pl.Buffered(1)),
        pl.BlockSpec(lb_raw.shape, lambda b, i: (0, 0, 0)),
        pl.BlockSpec((R, LANES), lambda b, i: (i, 0)),
        pl.BlockSpec((R, LANES), lambda b, i: (i, 0)),
    ]
    shapes = _inproj_lat_shapes(B, T)
    return pl.pallas_call(
        functools.partial(_inproj_kernel, False),
        grid=(B, T // R),
        in_specs=in_specs,
        out_specs=[pl.BlockSpec((1, R, s[2]), row_idx) for s, _ in shapes],
        out_shape=[jax.ShapeDtypeStruct(s, d) for s, d in shapes],
        compiler_params=pltpu.CompilerParams(dimension_semantics=("arbitrary", "arbitrary"),
                                             vmem_limit_bytes=VMEM_LIMIT),
        name="inproj_lat",
    )(x, mod_lat, nmix, win_p, qn, wq_p, kvn, wkv_p, lb_raw, cos_t, sin_t)


def _inproj_ctx_call(ctx, mod_ctx, nmix, win_p, kvn, wkv_p, lb_raw, tri_f, tri_b):
    B, L, D = ctx.shape
    R = PROJ_ROW_TILE
    per = R // L
    nq = MLA_HEADS * HEAD_SLOT
    x2 = ctx.reshape(B // per, R, D)
    const2 = lambda i: (0, 0)
    in_specs = [
        pl.BlockSpec((1, R, D), lambda i: (i, 0, 0)),
        pl.BlockSpec((1, 2, D), lambda i: (0, 0, 0)),
        pl.BlockSpec((1, D), const2),
        pl.BlockSpec((D, IN_P_WIDTH), const2, pipeline_mode=pl.Buffered(1)),
        pl.BlockSpec((1, MLA_KV_RANK), const2),
        pl.BlockSpec((MLA_KV_RANK, nq), const2, pipeline_mode=pl.Buffered(1)),
        pl.BlockSpec(lb_raw.shape, lambda i: (0, 0, 0)),
        pl.BlockSpec((L, L), const2),
        pl.BlockSpec((L, L), const2),
    ]
    kv_spec = pl.BlockSpec((per, L, nq), lambda i: (i, 0, 0))
    st_spec = pl.BlockSpec((per, HG_HEADS, HG_DV, HG_DK), lambda i: (i, 0, 0, 0))
    kv_shape = jax.ShapeDtypeStruct((B, L, nq), BF16)
    st_shape = jax.ShapeDtypeStruct((B, HG_HEADS, HG_DV, HG_DK), F32)
    return pl.pallas_call(
        functools.partial(_inproj_kernel, True),
        grid=(B // per,),
        in_specs=in_specs,
        out_specs=[kv_spec, kv_spec, st_spec, st_spec],
        out_shape=[kv_shape, kv_shape, st_shape, st_shape],
        compiler_params=pltpu.CompilerParams(dimension_semantics=("arbitrary",), vmem_limit_bytes=VMEM_LIMIT),
        name="inproj_ctx",
    )(x2, mod_ctx, nmix, win_p, kvn, wkv_p, lb_raw, tri_f, tri_b)


N_LEVELS = 8
N_SMALL_LEVELS = 2


def _neg_abs(x):
    bits = lax.bitcast_convert_type(x, jnp.uint32) | jnp.uint32(F32_SIGN_BIT)
    return lax.bitcast_convert_type(bits, F32)


def _replace_bit(y, row, bit, value):
    step = 1 << bit
    n = y.shape[0]
    has = ((row >> bit) & 1) == 1
    if value == 1:
        return jnp.where(has, y, pltpu.roll(y, n - step, 0))
    return jnp.where(has, pltpu.roll(y, step, 0), y)


def _cum_log2_decay(lf, tri):
    l_hi = lf.astype(BF16)
    r1 = lf - l_hi.astype(F32)
    l_mid = r1.astype(BF16)
    l_lo = (r1 - l_mid.astype(F32)).astype(BF16)
    return (_dot(tri, l_hi) + _dot(tri, l_mid) + _dot(tri, l_lo)) * LOG2E


def _hgrn_chunk_state(reverse, lf, vi, tri, st_out, n):
    b = _cum_log2_decay(lf, tri)
    end_row = 0 if reverse else lf.shape[0] - 1
    k_end = ((1.0 - jnp.exp(lf)) * jnp.exp2(b[end_row:end_row + 1] - b)).astype(BF16)
    for h in range(HG_HEADS):
        sl = slice(h * HG_DK, (h + 1) * HG_DK)
        st_out[n, h] = _dot_tn(vi[:, sl], k_end[:, sl])


def _hgrn_kernel(reverse, *refs):
    st0_ref, st_ref = refs[3], refs[-1]

    @pl.when(pl.program_id(1) == 0)
    def _():
        st_ref[...] = st0_ref[...].reshape(st_ref.shape)

    stages = [_hgrn_sample(reverse, n, *refs) for n in range(refs[0].shape[0])]
    while stages:
        stages = [s for s in stages if next(s, False)]


def _hgrn_sample(reverse, n, qh_ref, lf_ref, vi_ref, st0_ref, tri_ref, lv_ref, *rest):
    if reverse:
        of_ref, g_ref, hgn_ref, out_ref, st_ref = rest
    else:
        out_ref, st_ref = rest
    lf = lf_ref[n]
    vi_in = vi_ref[n]
    b_all = _cum_log2_decay(lf, tri_ref[...])
    f_all = jnp.exp(lf)
    k_all = 1.0 - f_all
    q_all = qh_ref[n].astype(F32)
    lv = lv_ref[...]
    R = lf.shape[0]
    H = R // 2
    row = lax.broadcasted_iota(jnp.int32, (R, LANES), 0)
    near, far = (0, 1) if not reverse else (1, 0)
    end_row = 0 if reverse else R - 1
    heads = range(HG_HEADS)
    sls = [slice(h * HG_DK, (h + 1) * HG_DK) for h in heads]
    bs = [b_all[:, sl] for sl in sls]
    qs = [q_all[:, sl] for sl in sls]
    ks = [k_all[:, sl] for sl in sls]
    fs = [f_all[:, sl] for sl in sls]
    vs = [vi_in[:, sl] for sl in sls]

    diag = lv == N_LEVELS - 1
    a0 = [jnp.where(diag, _dot_nt(qs[h][:H].astype(BF16), ks[h][:H].astype(BF16)), 0.0) for h in heads]
    a1 = [jnp.where(diag, _dot_nt(qs[h][H:].astype(BF16), ks[h][H:].astype(BF16)), 0.0) for h in heads]
    a_off = [None] * HG_HEADS
    zs = list(bs)
    for lev in range(N_LEVELS):
        m = 1 << lev
        mask = lv == lev
        q_side = ((row >> lev) & 1) == far
        for h in heads:
            bh, qh, kh = bs[h], qs[h], ks[h]
            if lev == 0:
                zs[h] = _replace_bit(zs[h], row, lev, far)
                u = jnp.where(q_side, qh * fs[h], kh).astype(BF16)
            elif lev < N_SMALL_LEVELS:
                bmid = _replace_bit(zs[h], row, lev, near)
                zs[h] = _replace_bit(zs[h], row, lev, far)
                qk = jnp.where(q_side, qh, kh)
                arg = _neg_abs(bh - bmid)
            elif 2 * m == SUBLANES:
                ref = m if reverse else m - 1
                bmid = jnp.concatenate([jnp.broadcast_to(bh[lo + ref:lo + ref + 1], (SUBLANES, HG_DK))
                                        for lo in range(0, R, SUBLANES)], axis=0)
                qk = jnp.where(q_side, qh, kh)
                arg = _neg_abs(bh - bmid)
            else:
                args, parts = [], []
                for j in range(R // (2 * m)):
                    lo = j * 2 * m
                    ref = lo + (m if reverse else m - 1)
                    bref = bh[ref:ref + 1]
                    if reverse:
                        args += [bh[lo:lo + m] - bref, bref - bh[lo + m:lo + 2 * m]]
                        parts += [qh[lo:lo + m], kh[lo + m:lo + 2 * m]]
                    else:
                        args += [bref - bh[lo:lo + m], bh[lo + m:lo + 2 * m] - bref]
                        parts += [kh[lo:lo + m], qh[lo + m:lo + 2 * m]]
                arg = jnp.concatenate(args, axis=0)
                qk = jnp.concatenate(parts, axis=0)
            if lev > 0:
                u = (qk * jnp.exp2(arg)).astype(BF16)
            if lev < N_LEVELS - 1:
                a0[h] = jnp.where(mask, _dot_nt(u[:H], u[:H]), a0[h])
                a1[h] = jnp.where(mask, _dot_nt(u[H:], u[H:]), a1[h])
            elif reverse:
                a_off[h] = _dot_nt(u[:H], u[H:])
            else:
                a_off[h] = _dot_nt(u[H:], u[:H])
        yield True

    for h in heads:
        sl, bh, qh, kh, vh = sls[h], bs[h], qs[h], ks[h], vs[h]
        zend = bh[end_row:end_row + 1]
        st = st_ref[n * HG_HEADS + h]
        if reverse:
            o_in = jnp.concatenate([
                _dot(jnp.concatenate([a0[h], a_off[h]], axis=1).astype(BF16), vh),
                _dot(a1[h].astype(BF16), vh[H:])], axis=0)
        else:
            o_in = jnp.concatenate([
                _dot(a0[h].astype(BF16), vh[:H]),
                _dot(jnp.concatenate([a_off[h], a1[h]], axis=1).astype(BF16), vh)], axis=0)
        o = o_in + _dot_nt((qh * jnp.exp2(bh)).astype(BF16), st.astype(BF16))
        k_end = (kh * jnp.exp2(zend - bh)).astype(BF16)
        st_ref[n * HG_HEADS + h] = jnp.exp2(zend) * st + _dot_tn(vh, k_end)

        if reverse:
            tot = of_ref[n, :, sl] + o
            out_ref[n, :, sl] = (_rms(tot, hgn_ref[...]) * g_ref[n, :, sl].astype(F32)).astype(out_ref.dtype)
        else:
            out_ref[n, :, sl] = o
        yield True


def _hgrn_call(reverse, lat, st0, tri, lv, extra=()):
    B, T, KW = lat[0].shape
    R = ROW_TILE
    n_lat = T // R

    if reverse:
        def lat_idx(b, i):
            return (b, n_lat - 1 - i, 0)
    else:
        def lat_idx(b, i):
            return (b, i, 0)

    NS = HGRN_SAMPLES
    const2 = lambda b, i: (0, 0)
    in_specs = [pl.BlockSpec((NS, R, KW), lat_idx)] * 3 + [
        pl.BlockSpec((NS, HG_HEADS, HG_DV, HG_DK), lambda b, i: (b, 0, 0, 0)),
        pl.BlockSpec((R, R), const2),
        pl.BlockSpec((R // 2, R // 2), const2),
    ]
    if reverse:
        in_specs += [
            pl.BlockSpec((NS, R, KW), lat_idx),
            pl.BlockSpec((NS, R, KW), lat_idx),
            pl.BlockSpec((1, HG_DV), const2),
        ]
        out_dtype = BF16
    else:
        out_dtype = F32
    return pl.pallas_call(
        functools.partial(_hgrn_kernel, reverse),
        grid=(B // NS, n_lat),
        in_specs=in_specs,
        out_specs=pl.BlockSpec((NS, R, KW), lat_idx),
        out_shape=jax.ShapeDtypeStruct((B, T, KW), out_dtype),
        scratch_shapes=[pltpu.VMEM((NS * HG_HEADS, HG_DV, HG_DK), F32)],
        compiler_params=pltpu.CompilerParams(dimension_semantics=("arbitrary", "arbitrary"),
                                             vmem_limit_bytes=VMEM_LIMIT),
        name="hgrn_bwd" if reverse else "hgrn_fwd",
    )(*lat, st0, tri, lv, *extra)


def _attn_kernel(q_ref, k_ref, kc_ref, v_ref, vc_ref, o_ref):
    n_lat = k_ref.shape[1]

    def scores(h):
        sl = slice(h * HEAD_SLOT, (h + 1) * HEAD_SLOT)
        qh = q_ref[0, :, sl]
        return jnp.concatenate([_dot_nt(qh, k_ref[0, :, sl]), _dot_nt(qh, kc_ref[0, :, sl])], axis=1)

    def attend(h, s):
        sl = slice(h * HEAD_SLOT, (h + 1) * HEAD_SLOT)
        pb = jnp.exp2(s - jnp.max(s, axis=-1, keepdims=True)).astype(BF16)
        o = _dot(pb[:, :n_lat], v_ref[0, :, sl]) + _dot(pb[:, n_lat:], vc_ref[0, :, sl])
        return o * (1.0 / o[:, MLA_V:MLA_V + 1])

    outs = []
    s_next = scores(0)
    for h in range(MLA_HEADS):
        s = s_next
        if h + 1 < MLA_HEADS:
            s_next = scores(h + 1)
        outs.append(attend(h, s))
    first = _lane_lt(q_ref.shape[1], MLA_V)
    pairs = [jnp.where(first, outs[2 * j], pltpu.roll(outs[2 * j + 1], MLA_V, 1)) for j in range(MLA_HEADS // 2)]
    o_ref[0] = jnp.concatenate(pairs, axis=-1).astype(o_ref.dtype)


def _attn_call(q, k, k_ctx, v, v_ctx):
    B, T, nq = q.shape
    R = ATTN_ROW_TILE
    whole = lambda a: pl.BlockSpec((1,) + a.shape[1:], lambda b, i: (b, 0, 0))
    return pl.pallas_call(
        _attn_kernel,
        grid=(B, T // R),
        in_specs=[pl.BlockSpec((1, R, nq), lambda b, i: (b, i, 0)), whole(k), whole(k_ctx), whole(v), whole(v_ctx)],
        out_specs=pl.BlockSpec((1, R, MLA_HEADS * MLA_V), lambda b, i: (b, i, 0)),
        out_shape=jax.ShapeDtypeStruct((B, T, MLA_HEADS * MLA_V), BF16),
        compiler_params=pltpu.CompilerParams(dimension_semantics=("arbitrary", "arbitrary"),
                                             vmem_limit_bytes=VMEM_LIMIT),
        name="mla_attn",
    )(q, k, k_ctx, v, v_ctx)


def _mlp_kernel(n_ff_chunks, x_ref, om_ref, oh_ref, mod_ref, woa_ref, wob_ref, nmlp_ref, w1_ref, w2_ref,
                fn_ref, o_ref):
    mod = mod_ref[0]
    y = _dot(om_ref[0], woa_ref[...]) + _dot(oh_ref[0], wob_ref[...])
    x1 = x_ref[0] + mod[0:1] * y
    h2 = (_rms(x1, nmlp_ref[...]) * (1.0 + mod[2:3]) + mod[1:2]).astype(BF16)
    ff = w1_ref.shape[1] // n_ff_chunks
    acc = jnp.zeros(x1.shape, F32)
    for c in range(n_ff_chunks):
        u = jnp.maximum(_dot(h2, w1_ref[:, c * ff:(c + 1) * ff]), 0.0)
        acc = acc + _dot((u * u).astype(BF16), w2_ref[c * ff:(c + 1) * ff, :])
    x2 = x1 + mod[3:4] * acc
    o_ref[0] = _rms(x2, fn_ref[...])


def _mlp_call(x, o_mla, o_hg, mod4, woa, wob, nmlp, w1, w2, fnorm):
    B, T, D = x.shape
    R = MLP_ROW_TILE
    dff = w1.shape[1]
    half = o_mla.shape[2]
    row_idx = lambda b, i: (b, i, 0)
    const2 = lambda b, i: (0, 0)
    return pl.pallas_call(
        functools.partial(_mlp_kernel, MLP_FF_CHUNKS),
        grid=(B, T // R),
        in_specs=[
            pl.BlockSpec((1, R, D), row_idx),
            pl.BlockSpec((1, R, half), row_idx),
            pl.BlockSpec((1, R, half), row_idx),
            pl.BlockSpec((1, 4, D), lambda b, i: (b, 0, 0)),
            pl.BlockSpec((half, D), const2, pipeline_mode=pl.Buffered(1)),
            pl.BlockSpec((half, D), const2, pipeline_mode=pl.Buffered(1)),
            pl.BlockSpec((1, D), const2),
            pl.BlockSpec((D, dff), const2, pipeline_mode=pl.Buffered(1)),
            pl.BlockSpec((dff, D), const2, pipeline_mode=pl.Buffered(1)),
            pl.BlockSpec((1, D), const2),
        ],
        out_specs=pl.BlockSpec((1, R, D), row_idx),
        out_shape=jax.ShapeDtypeStruct((B, T, D), F32),
        compiler_params=pltpu.CompilerParams(dimension_semantics=("arbitrary", "arbitrary"),
                                             vmem_limit_bytes=VMEM_LIMIT),
        name="out_mlp",
    )(x, o_mla, o_hg, mod4, woa, wob, nmlp, w1, w2, fnorm)


def _rope_slot(w):
    z = jnp.zeros(w.shape[:-1] + (MLA_NOPE,), w.dtype)
    z2 = jnp.zeros(w.shape[:-1] + (HEAD_SLOT - MLA_NOPE - MLA_ROPE,), w.dtype)
    return jnp.concatenate([z, w, z2], axis=-1)


def _pack_w_in(w_in):
    w = w_in.astype(BF16)
    n_mla = MLA_Q_RANK + MLA_KV_RANK
    return jnp.concatenate([w[:, :n_mla], _rope_slot(w[:, n_mla:n_mla + MLA_ROPE]), w[:, n_mla + MLA_ROPE:]], axis=1)


def _pack_w_uq(w_uq):
    k = w_uq.shape[0]
    w3 = w_uq.reshape(k, MLA_HEADS, MLA_NOPE + MLA_ROPE)
    nope = w3[..., :MLA_NOPE].reshape(k, MLA_HEADS * MLA_NOPE)
    rope = w3[..., MLA_NOPE:].reshape(k, MLA_HEADS // 2, 2 * MLA_ROPE)
    rope = jnp.concatenate([jnp.zeros((k, MLA_HEADS // 2, LANES - 2 * MLA_ROPE), w_uq.dtype), rope], axis=-1)
    return jnp.concatenate([nope, rope.reshape(k, (MLA_HEADS // 2) * LANES)], axis=1).astype(BF16)


def _pack_w_ukv(w_ukv):
    k = w_ukv.shape[0]
    w3 = w_ukv.reshape(k, MLA_HEADS, MLA_NOPE + MLA_V)
    kk = w3[..., :MLA_NOPE].reshape(k, MLA_HEADS * MLA_NOPE)
    vv = w3[..., MLA_NOPE:].reshape(k, MLA_HEADS * MLA_V)
    return jnp.concatenate([kk, vv], axis=1).astype(BF16)


def _rope_tables(T):
    rows = T // GRID_W
    row = np.repeat(np.arange(rows), GRID_W).astype(np.float64)
    col = np.tile(np.arange(GRID_W), rows).astype(np.float64)
    n = MLA_ROPE // 4
    inv = ROPE_BASE ** (-np.arange(n, dtype=np.float64) / n)
    ar, ac = row[:, None] * inv, col[:, None] * inv
    cos32 = np.concatenate([np.cos(ar), np.cos(ar), np.cos(ac), np.cos(ac)], axis=-1)
    sin32 = np.concatenate([np.sin(ar), np.sin(ar), np.sin(ac), np.sin(ac)], axis=-1)
    cos_t = np.concatenate([np.ones((T, MLA_NOPE)), cos32, cos32], axis=-1)
    sin_t = np.concatenate([np.zeros((T, MLA_NOPE)), sin32, sin32], axis=-1)
    return jnp.asarray(cos_t, F32), jnp.asarray(sin_t, F32)


def _scan_constants(reverse):
    R = ROW_TILE
    t = np.arange(R, dtype=np.int32)[:, None]
    s = np.arange(R, dtype=np.int32)[None, :]
    if reverse:
        t, s = s, t
    tri = (t >= s)
    H = R // 2
    t, s = np.broadcast_to(t, (R, R))[:H, :H], np.broadcast_to(s, (R, R))[:H, :H]
    x = t ^ s
    lev = np.zeros((H, H), np.int32)
    for bit in range(1, N_LEVELS - 1):
        lev = np.where((x >> bit) > 0, bit, lev)
    lv = np.where(t > s, lev, np.where(t == s, N_LEVELS - 1, -1))
    return jnp.asarray(tri, BF16), jnp.asarray(lv, jnp.int32)


def kernel(x, c, ctx, c_ctx, w_ada, b_ada, norm_mix, w_in, q_norm, w_uq, kv_norm, w_ukv, hgrn_lb, hgrn_norm,
           w_out, norm_mlp, w_mlp_in, w_mlp_out, final_norm):
    B, T, D = x.shape
    L = ctx.shape[1]
    assert w_ada.shape[0] == 1, "single-layer block"
    assert T % ROW_TILE == 0 and L == ROW_TILE and T % GRID_W == 0
    assert T % MLP_ROW_TILE == 0 and T % ATTN_ROW_TILE == 0 and T % PROJ_ROW_TILE == 0
    assert PROJ_ROW_TILE % L == 0 and B % (PROJ_ROW_TILE // L) == 0 and T % L == 0 and B % HGRN_SAMPLES == 0

    pad_rows = (-(B + 1)) % 8
    cc = jnp.concatenate([c, c_ctx[None, :], jnp.zeros((pad_rows, D), F32)], axis=0)
    mod = _ada_call(cc, w_ada[0], b_ada[0][None, :])
    mod_lat = mod[:B].reshape(B, 6, D)
    mod_ctx = mod[B].reshape(1, 6, D)
    mod4 = mod_lat[:, 2:6]

    cos_t, sin_t = _rope_tables(T)
    win_p, wkv_p = _pack_w_in(w_in[0]), _pack_w_ukv(w_ukv[0])
    q, k, v, qh, lff, lfb, vi, g = _inproj_lat_call(
        x, mod_lat[:, 0:2], norm_mix, win_p, q_norm, _pack_w_uq(w_uq[0]), kv_norm, wkv_p, hgrn_lb, cos_t, sin_t)
    tri_f, lv_f = _scan_constants(False)
    tri_b, lv_b = _scan_constants(True)
    kc, vc, st_f, st_b = _inproj_ctx_call(
        ctx, mod_ctx[:, 0:2], norm_mix, win_p, kv_norm, wkv_p, hgrn_lb, tri_f, tri_b)

    o_f = _hgrn_call(False, (qh, lff, vi), st_f, tri_f, lv_f)
    o_hg = _hgrn_call(True, (qh, lfb, vi), st_b, tri_b, lv_b, extra=(o_f, g, hgrn_norm))

    o_mla = _attn_call(q, k, kc, v, vc)

    half = MLA_HEADS * MLA_V
    wo = w_out[0].astype(BF16)
    return _mlp_call(x, o_mla, o_hg, mod4, wo[:half], wo[half:], norm_mlp, w_mlp_in[0].astype(BF16),
                     w_mlp_out[0].astype(BF16), final_norm[None, :])
```

```python
import functools

import jax
import jax.numpy as jnp
import numpy as np
from jax import lax
from jax.experimental import pallas as pl
from jax.experimental.pallas import tpu as pltpu

F32 = jnp.float32
BF16 = jnp.bfloat16

GRID_W = 64
MLA_HEADS = 8
MLA_Q_RANK = 256
MLA_KV_RANK = 128
MLA_NOPE = 64
MLA_ROPE = 32
MLA_V = 64
HG_HEADS = 4
HG_DK = 128
HG_DV = 128
HG_KW = HG_HEADS * HG_DK
ROPE_BASE = 10000.0
EPS = 1e-6
LOG2E = 1.4426950408889634

LANES = 128
SUBLANES = 8
HEAD_SLOT = LANES
ROW_TILE = 256
HGRN_SAMPLES = 4
MLP_ROW_TILE = 1024
MLP_FF_CHUNKS = 4
ADA_COL_TILE = 1024
F32_SIGN_BIT = 0x80000000
PROJ_ROW_TILE = 1024
ATTN_ROW_TILE = 512
VMEM_LIMIT = 56 * 1024 * 1024

_C_CQ = (0, 256)
_C_CKV = (256, 384)
_C_KR = (384, 512)
_C_HQ = (512, 1024)
_C_HF = (1024, 1536)
_C_HB = (1536, 2048)
_C_HI = (2048, 2560)
_C_HG = (2560, 3072)
IN_P_WIDTH = 3072


def _rms(x, g):
    ms = jnp.mean(x * x, axis=-1, keepdims=True)
    return (x * lax.rsqrt(ms + EPS)) * g


def _silu(x):
    return x * jax.nn.sigmoid(x)


def _dot(a, b):
    return jnp.dot(a, b, preferred_element_type=F32)


def _dot_nt(a, b):
    return lax.dot_general(a, b, (((1,), (1,)), ((), ())), preferred_element_type=F32)


def _dot_tn(a, b):
    return lax.dot_general(a, b, (((0,), (0,)), ((), ())), preferred_element_type=F32)


def _ada_kernel(c_ref, w_ref, b_ref, o_ref):
    s = _silu(c_ref[...])
    o_ref[...] = jnp.dot(s, w_ref[...], preferred_element_type=F32,
                         precision=lax.Precision.HIGHEST) + b_ref[...]


def _ada_call(cc, w_ada, b_ada):
    rows, d = cc.shape
    n = w_ada.shape[1]
    tn = ADA_COL_TILE
    return pl.pallas_call(
        _ada_kernel,
        grid=(n // tn,),
        in_specs=[
            pl.BlockSpec((rows, d), lambda j: (0, 0)),
            pl.BlockSpec((d, tn), lambda j: (0, j)),
            pl.BlockSpec((1, tn), lambda j: (0, j)),
        ],
        out_specs=pl.BlockSpec((rows, tn), lambda j: (0, j)),
        out_shape=jax.ShapeDtypeStruct((rows, n), F32),
        compiler_params=pltpu.CompilerParams(dimension_semantics=("arbitrary",), vmem_limit_bytes=VMEM_LIMIT),
        name="ada_mod",
    )(cc, w_ada, b_ada)


def _lane_lt(n_rows, bound):
    return lax.broadcasted_iota(jnp.int32, (n_rows, LANES), 1) < bound


def _rotate_half_lanes(x):
    lane = lax.broadcasted_iota(jnp.int32, x.shape, 1)
    n = MLA_ROPE // 4
    return jnp.where((lane & (2 * n - 1)) < n, -pltpu.roll(x, LANES - n, 1), pltpu.roll(x, n, 1))


def _head_slots(nope, rope_lo, rope_hi):
    first = _lane_lt(nope.shape[0], MLA_NOPE)
    return (jnp.where(first, nope, rope_lo), jnp.where(first, pltpu.roll(nope, MLA_NOPE, 1), rope_hi))


def _inproj_kernel(is_ctx, *refs):
    if is_ctx:
        (x_ref, mod_ref, nmix_ref, win_ref, kvn_ref, wkv_ref, lb_ref, trif_ref, trib_ref,
         k_out, v_out, stf_out, stb_out) = refs
    else:
        (x_ref, mod_ref, nmix_ref, win_ref, qn_ref, wq_ref, kvn_ref, wkv_ref, lb_ref, cos_ref, sin_ref,
         q_out, k_out, v_out, qh_out, lff_out, lfb_out, vi_out, g_out) = refs
    mod = mod_ref[0]
    h = _rms(x_ref[0], nmix_ref[...]) * (1.0 + mod[1:2]) + mod[0:1]
    hb = h.astype(BF16)
    n_pairs = MLA_HEADS // 2
    half = n_pairs * LANES

    def proj(c):
        return _dot(hb, win_ref[:, c[0]:c[1]])

    def put(ref, val):
        ref[...] = val.astype(ref.dtype).reshape(ref.shape)

    def pair(a, j):
        return a[:, j * LANES:(j + 1) * LANES]

    lbr = lb_ref[...]
    e = jnp.exp(lbr - jnp.max(lbr, axis=0, keepdims=True))
    lb = e[0] / jnp.sum(e, axis=0)
    lbf = lb[0:1]
    lbb = lb[1:2]

    ckv = proj(_C_CKV)
    kr = proj(_C_KR)
    if not is_ctx:
        cq = proj(_C_CQ)
    lf_f = jnp.log(lbf + (1.0 - lbf) * jax.nn.sigmoid(proj(_C_HF)))
    vi = proj(_C_HI).astype(BF16)
    kv = _dot(_rms(ckv, kvn_ref[...]).astype(BF16), wkv_ref[...])
    lf_b = jnp.log(lbb + (1.0 - lbb) * jax.nn.sigmoid(proj(_C_HB)))
    if is_ctx:
        rows = x_ref.shape[1] // stf_out.shape[0]
        for s in range(stf_out.shape[0]):
            rs = slice(s * rows, (s + 1) * rows)
            for reverse, lf, tri_ref, st_out in ((False, lf_f, trif_ref, stf_out), (True, lf_b, trib_ref, stb_out)):
                _hgrn_chunk_state(reverse, lf[rs], vi[rs], tri_ref[...], st_out, s)
    else:
        put(lff_out, lf_f)
        put(vi_out, vi)
        put(lfb_out, lf_b)
    if not is_ctx:
        cos = cos_ref[...]
        sin = sin_ref[...]
        kr = kr * cos + _rotate_half_lanes(kr) * sin
        qq = _dot(_rms(cq, qn_ref[...]).astype(BF16), wq_ref[...])
    if not is_ctx:
        put(qh_out, _silu(proj(_C_HQ)))
    k_slots = []
    for j in range(n_pairs):
        k_slots += _head_slots(pair(kv, j), kr, kr)
    put(k_out, jnp.concatenate(k_slots, axis=1))
    lane = lax.broadcasted_iota(jnp.int32, (kv.shape[0], LANES), 1)
    ones_lane = jnp.where(lane == MLA_V, 1.0, 0.0)
    v_slots = []
    for j in range(n_pairs):
        v_slots += _head_slots(pair(kv, n_pairs + j), ones_lane, ones_lane)
    put(v_out, jnp.concatenate(v_slots, axis=1))
    if not is_ctx:
        scale = (MLA_NOPE + MLA_ROPE) ** -0.5 * LOG2E
        q_slots = []
        for j in range(n_pairs):
            r = pair(qq, n_pairs + j)
            r = (r * cos + _rotate_half_lanes(r) * sin) * scale
            q_slots += _head_slots(pair(qq, j) * scale, r, pltpu.roll(r, LANES - MLA_ROPE, 1))
        put(q_out, jnp.concatenate(q_slots, axis=1))
        put(g_out, _silu(proj(_C_HG)))


def _inproj_lat_shapes(B, rows):
    nq = MLA_HEADS * HEAD_SLOT
    return [
        ((B, rows, nq), BF16),
        ((B, rows, nq), BF16),
        ((B, rows, nq), BF16),
        ((B, rows, HG_KW), BF16),
        ((B, rows, HG_KW), F32),
        ((B, rows, HG_KW), F32),
        ((B, rows, HG_KW), BF16),
        ((B, rows, HG_KW), BF16),
    ]


def _inproj_lat_call(x, mod_lat, nmix, win_p, qn, wq_p, kvn, wkv_p, lb_raw, cos_t, sin_t):
    B, T, D = x.shape
    R = PROJ_ROW_TILE
    nq = MLA_HEADS * HEAD_SLOT
    row_idx = lambda b, i: (b, i, 0)
    const2 = lambda b, i: (0, 0)
    in_specs = [
        pl.BlockSpec((1, R, D), row_idx),
        pl.BlockSpec((1, 2, D), lambda b, i: (b, 0, 0)),
        pl.BlockSpec((1, D), const2),
        pl.BlockSpec((D, IN_P_WIDTH), const2, pipeline_mode=pl.Buffered(1)),
        pl.BlockSpec((1, MLA_Q_RANK), const2),
        pl.BlockSpec((MLA_Q_RANK, nq), const2, pipeline_mode=pl.Buffered(1)),
        pl.BlockSpec((1, MLA_KV_RANK), const2),
        pl.BlockSpec((MLA_KV_RANK, nq), const2, pipeline_mode=pl.Buffered(1)),
        pl.BlockSpec(lb_raw.shape, lambda b, i: (0, 0, 0)),
        pl.BlockSpec((R, LANES), lambda b, i: (i, 0)),
        pl.BlockSpec((R, LANES), lambda b, i: (i, 0)),
    ]
    shapes = _inproj_lat_shapes(B, T)
    return pl.pallas_call(
        functools.partial(_inproj_kernel, False),
        grid=(B, T // R),
        in_specs=in_specs,
        out_specs=[pl.BlockSpec((1, R, s[2]), row_idx) for s, _ in shapes],
        out_shape=[jax.ShapeDtypeStruct(s, d) for s, d in shapes],
        compiler_params=pltpu.CompilerParams(dimension_semantics=("arbitrary", "arbitrary"),
                                             vmem_limit_bytes=VMEM_LIMIT),
        name="inproj_lat",
    )(x, mod_lat, nmix, win_p, qn, wq_p, kvn, wkv_p, lb_raw, cos_t, sin_t)


def _inproj_ctx_call(ctx, mod_ctx, nmix, win_p, kvn, wkv_p, lb_raw, tri_f, tri_b):
    B, L, D = ctx.shape
    R = PROJ_ROW_TILE
    per = R // L
    nq = MLA_HEADS * HEAD_SLOT
    x2 = ctx.reshape(B // per, R, D)
    const2 = lambda i: (0, 0)
    in_specs = [
        pl.BlockSpec((1, R, D), lambda i: (i, 0, 0)),
        pl.BlockSpec((1, 2, D), lambda i: (0, 0, 0)),
        pl.BlockSpec((1, D), const2),
        pl.BlockSpec((D, IN_P_WIDTH), const2, pipeline_mode=pl.Buffered(1)),
        pl.BlockSpec((1, MLA_KV_RANK), const2),
        pl.BlockSpec((MLA_KV_RANK, nq), const2, pipeline_mode=pl.Buffered(1)),
        pl.BlockSpec(lb_raw.shape, lambda i: (0, 0, 0)),
        pl.BlockSpec((L, L), const2),
        pl.BlockSpec((L, L), const2),
    ]
    kv_spec = pl.BlockSpec((per, L, nq), lambda i: (i, 0, 0))
    st_spec = pl.BlockSpec((per, HG_HEADS, HG_DV, HG_DK), lambda i: (i, 0, 0, 0))
    kv_shape = jax.ShapeDtypeStruct((B, L, nq), BF16)
    st_shape = jax.ShapeDtypeStruct((B, HG_HEADS, HG_DV, HG_DK), F32)
    return pl.pallas_call(
        functools.partial(_inproj_kernel, True),
        grid=(B // per,),
        in_specs=in_specs,
        out_specs=[kv_spec, kv_spec, st_spec, st_spec],
        out_shape=[kv_shape, kv_shape, st_shape, st_shape],
        compiler_params=pltpu.CompilerParams(dimension_semantics=("arbitrary",), vmem_limit_bytes=VMEM_LIMIT),
        name="inproj_ctx",
    )(x2, mod_ctx, nmix, win_p, kvn, wkv_p, lb_raw, tri_f, tri_b)


N_LEVELS = 8
N_SMALL_LEVELS = 2


def _neg_abs(x):
    bits = lax.bitcast_convert_type(x, jnp.uint32) | jnp.uint32(F32_SIGN_BIT)
    return lax.bitcast_convert_type(bits, F32)


def _replace_bit(y, row, bit, value):
    step = 1 << bit
    n = y.shape[0]
    has = ((row >> bit) & 1) == 1
    if value == 1:
        return jnp.where(has, y, pltpu.roll(y, n - step, 0))
    return jnp.where(has, pltpu.roll(y, step, 0), y)


def _cum_log2_decay(lf, tri):
    l_hi = lf.astype(BF16)
    r1 = lf - l_hi.astype(F32)
    l_mid = r1.astype(BF16)
    l_lo = (r1 - l_mid.astype(F32)).astype(BF16)
    w = lf.shape[1]
    s3 = _dot(tri, jnp.concatenate([l_hi, l_mid, l_lo], axis=1))
    return (s3[:, :w] + s3[:, w:2 * w] + s3[:, 2 * w:]) * LOG2E


def _hgrn_chunk_state(reverse, lf, vi, tri, st_out, n):
    b = _cum_log2_decay(lf, tri)
    end_row = 0 if reverse else lf.shape[0] - 1
    k_end = ((1.0 - jnp.exp(lf)) * jnp.exp2(b[end_row:end_row + 1] - b)).astype(BF16)
    for h in range(HG_HEADS):
        sl = slice(h * HG_DK, (h + 1) * HG_DK)
        st_out[n, h] = _dot_tn(vi[:, sl], k_end[:, sl])


def _hgrn_kernel(reverse, *refs):
    st0_ref, st_ref = refs[3], refs[-1]

    @pl.when(pl.program_id(1) == 0)
    def _():
        st_ref[...] = st0_ref[...].reshape(st_ref.shape)

    stages = [_hgrn_sample(reverse, n, *refs) for n in range(refs[0].shape[0])]
    while stages:
        stages = [s for s in stages if next(s, False)]


def _hgrn_sample(reverse, n, qh_ref, lf_ref, vi_ref, st0_ref, tri_ref, lv_ref, *rest):
    if reverse:
        of_ref, g_ref, hgn_ref, out_ref, st_ref = rest
    else:
        out_ref, st_ref = rest
    lf = lf_ref[n]
    vi_in = vi_ref[n]
    b_all = _cum_log2_decay(lf, tri_ref[...])
    f_all = jnp.exp(lf)
    k_all = 1.0 - f_all
    q_all = qh_ref[n].astype(F32)
    lv = lv_ref[...]
    R = lf.shape[0]
    H = R // 2
    row = lax.broadcasted_iota(jnp.int32, (R, LANES), 0)
    near, far = (0, 1) if not reverse else (1, 0)
    end_row = 0 if reverse else R - 1
    heads = range(HG_HEADS)
    sls = [slice(h * HG_DK, (h + 1) * HG_DK) for h in heads]
    bs = [b_all[:, sl] for sl in sls]
    qs = [q_all[:, sl] for sl in sls]
    ks = [k_all[:, sl] for sl in sls]
    fs = [f_all[:, sl] for sl in sls]
    vs = [vi_in[:, sl] for sl in sls]

    diag = lv == N_LEVELS - 1
    qk_raw = [_dot_nt(qs[h].astype(BF16), ks[h].astype(BF16)) for h in heads]
    a0 = [jnp.where(diag, qk_raw[h][:H, :H], 0.0) for h in heads]
    a1 = [jnp.where(diag, qk_raw[h][H:, H:], 0.0) for h in heads]
    a_off = [None] * HG_HEADS
    zs = list(bs)
    for lev in range(N_LEVELS):
        m = 1 << lev
        mask = lv == lev
        q_side = ((row >> lev) & 1) == far
        for h in heads:
            bh, qh, kh = bs[h], qs[h], ks[h]
            if lev == 0:
                zs[h] = _replace_bit(zs[h], row, lev, far)
                u = jnp.where(q_side, qh * fs[h], kh).astype(BF16)
            elif lev < N_SMALL_LEVELS:
                bmid = _replace_bit(zs[h], row, lev, near)
                zs[h] = _replace_bit(zs[h], row, lev, far)
                qk = jnp.where(q_side, qh, kh)
                arg = _neg_abs(bh - bmid)
            elif 2 * m == SUBLANES:
                ref = m if reverse else m - 1
                bmid = jnp.concatenate([jnp.broadcast_to(bh[lo + ref:lo + ref + 1], (SUBLANES, HG_DK))
                                        for lo in range(0, R, SUBLANES)], axis=0)
                qk = jnp.where(q_side, qh, kh)
                arg = _neg_abs(bh - bmid)
            else:
                args, parts = [], []
                for j in range(R // (2 * m)):
                    lo = j * 2 * m
                    ref = lo + (m if reverse else m - 1)
                    bref = bh[ref:ref + 1]
                    if reverse:
                        args += [bh[lo:lo + m] - bref, bref - bh[lo + m:lo + 2 * m]]
                        parts += [qh[lo:lo + m], kh[lo + m:lo + 2 * m]]
                    else:
                        args += [bref - bh[lo:lo + m], bh[lo + m:lo + 2 * m] - bref]
                        parts += [kh[lo:lo + m], qh[lo + m:lo + 2 * m]]
                arg = jnp.concatenate(args, axis=0)
                qk = jnp.concatenate(parts, axis=0)
            if lev > 0:
                u = (qk * jnp.exp2(arg)).astype(BF16)
            p = _dot_nt(u, u)
            if lev < N_LEVELS - 1:
                a0[h] = jnp.where(mask, p[:H, :H], a0[h])
                a1[h] = jnp.where(mask, p[H:, H:], a1[h])
            elif reverse:
                a_off[h] = p[:H, H:]
            else:
                a_off[h] = p[H:, :H]
        yield True

    for h in heads:
        sl, bh, qh, kh, vh = sls[h], bs[h], qs[h], ks[h], vs[h]
        zend = bh[end_row:end_row + 1]
        st = st_ref[n * HG_HEADS + h]
        zero = jnp.zeros((H, H), F32)
        if reverse:
            a_full = jnp.concatenate([jnp.concatenate([a0[h], a_off[h]], axis=1),
                                      jnp.concatenate([zero, a1[h]], axis=1)], axis=0)
        else:
            a_full = jnp.concatenate([jnp.concatenate([a0[h], zero], axis=1),
                                      jnp.concatenate([a_off[h], a1[h]], axis=1)], axis=0)
        o = _dot(a_full.astype(BF16), vh) + _dot_nt((qh * jnp.exp2(bh)).astype(BF16), st.astype(BF16))
        k_end = (kh * jnp.exp2(zend - bh)).astype(BF16)
        st_ref[n * HG_HEADS + h] = jnp.exp2(zend) * st + _dot_tn(vh, k_end)

        if reverse:
            tot = of_ref[n, :, sl] + o
            out_ref[n, :, sl] = (_rms(tot, hgn_ref[...]) * g_ref[n, :, sl].astype(F32)).astype(out_ref.dtype)
        else:
            out_ref[n, :, sl] = o
        yield True


def _hgrn_call(reverse, lat, st0, tri, lv, extra=()):
    B, T, KW = lat[0].shape
    R = ROW_TILE
    n_lat = T // R

    if reverse:
        def lat_idx(b, i):
            return (b, n_lat - 1 - i, 0)
    else:
        def lat_idx(b, i):
            return (b, i, 0)

    NS = HGRN_SAMPLES
    const2 = lambda b, i: (0, 0)
    in_specs = [pl.BlockSpec((NS, R, KW), lat_idx)] * 3 + [
        pl.BlockSpec((NS, HG_HEADS, HG_DV, HG_DK), lambda b, i: (b, 0, 0, 0)),
        pl.BlockSpec((R, R), const2),
        pl.BlockSpec((R // 2, R // 2), const2),
    ]
    if reverse:
        in_specs += [
            pl.BlockSpec((NS, R, KW), lat_idx),
            pl.BlockSpec((NS, R, KW), lat_idx),
            pl.BlockSpec((1, HG_DV), const2),
        ]
        out_dtype = BF16
    else:
        out_dtype = F32
    return pl.pallas_call(
        functools.partial(_hgrn_kernel, reverse),
        grid=(B // NS, n_lat),
        in_specs=in_specs,
        out_specs=pl.BlockSpec((NS, R, KW), lat_idx),
        out_shape=jax.ShapeDtypeStruct((B, T, KW), out_dtype),
        scratch_shapes=[pltpu.VMEM((NS * HG_HEADS, HG_DV, HG_DK), F32)],
        compiler_params=pltpu.CompilerParams(dimension_semantics=("arbitrary", "arbitrary"),
                                             vmem_limit_bytes=VMEM_LIMIT),
        name="hgrn_bwd" if reverse else "hgrn_fwd",
    )(*lat, st0, tri, lv, *extra)


def _attn_kernel(q_ref, k_ref, kc_ref, v_ref, vc_ref, o_ref):
    n_lat = k_ref.shape[1]

    def scores(h):
        sl = slice(h * HEAD_SLOT, (h + 1) * HEAD_SLOT)
        qh = q_ref[0, :, sl]
        return jnp.concatenate([_dot_nt(qh, k_ref[0, :, sl]), _dot_nt(qh, kc_ref[0, :, sl])], axis=1)

    def attend(h, s):
        sl = slice(h * HEAD_SLOT, (h + 1) * HEAD_SLOT)
        pb = jnp.exp2(s - jnp.max(s, axis=-1, keepdims=True)).astype(BF16)
        o = _dot(pb[:, :n_lat], v_ref[0, :, sl]) + _dot(pb[:, n_lat:], vc_ref[0, :, sl])
        return o * (1.0 / o[:, MLA_V:MLA_V + 1])

    outs = []
    s_next = scores(0)
    for h in range(MLA_HEADS):
        s = s_next
        if h + 1 < MLA_HEADS:
            s_next = scores(h + 1)
        outs.append(attend(h, s))
    first = _lane_lt(q_ref.shape[1], MLA_V)
    pairs = [jnp.where(first, outs[2 * j], pltpu.roll(outs[2 * j + 1], MLA_V, 1)) for j in range(MLA_HEADS // 2)]
    o_ref[0] = jnp.concatenate(pairs, axis=-1).astype(o_ref.dtype)


def _attn_call(q, k, k_ctx, v, v_ctx):
    B, T, nq = q.shape
    R = ATTN_ROW_TILE
    whole = lambda a: pl.BlockSpec((1,) + a.shape[1:], lambda b, i: (b, 0, 0))
    return pl.pallas_call(
        _attn_kernel,
        grid=(B, T // R),
        in_specs=[pl.BlockSpec((1, R, nq), lambda b, i: (b, i, 0)), whole(k), whole(k_ctx), whole(v), whole(v_ctx)],
        out_specs=pl.BlockSpec((1, R, MLA_HEADS * MLA_V), lambda b, i: (b, i, 0)),
        out_shape=jax.ShapeDtypeStruct((B, T, MLA_HEADS * MLA_V), BF16),
        compiler_params=pltpu.CompilerParams(dimension_semantics=("arbitrary", "arbitrary"),
                                             vmem_limit_bytes=VMEM_LIMIT),
        name="mla_attn",
    )(q, k, k_ctx, v, v_ctx)


def _mlp_kernel(n_ff_chunks, x_ref, om_ref, oh_ref, mod_ref, woa_ref, wob_ref, nmlp_ref, w1_ref, w2_ref,
                fn_ref, o_ref):
    mod = mod_ref[0]
    y = _dot(om_ref[0], woa_ref[...]) + _dot(oh_ref[0], wob_ref[...])
    x1 = x_ref[0] + mod[0:1] * y
    h2 = (_rms(x1, nmlp_ref[...]) * (1.0 + mod[2:3]) + mod[1:2]).astype(BF16)
    ff = w1_ref.shape[1] // n_ff_chunks
    acc = jnp.zeros(x1.shape, F32)
    for c in range(n_ff_chunks):
        u = jnp.maximum(_dot(h2, w1_ref[:, c * ff:(c + 1) * ff]), 0.0)
        acc = acc + _dot((u * u).astype(BF16), w2_ref[c * ff:(c + 1) * ff, :])
    x2 = x1 + mod[3:4] * acc
    o_ref[0] = _rms(x2, fn_ref[...])


def _mlp_call(x, o_mla, o_hg, mod4, woa, wob, nmlp, w1, w2, fnorm):
    B, T, D = x.shape
    R = MLP_ROW_TILE
    dff = w1.shape[1]
    half = o_mla.shape[2]
    row_idx = lambda b, i: (b, i, 0)
    const2 = lambda b, i: (0, 0)
    return pl.pallas_call(
        functools.partial(_mlp_kernel, MLP_FF_CHUNKS),
        grid=(B, T // R),
        in_specs=[
            pl.BlockSpec((1, R, D), row_idx),
            pl.BlockSpec((1, R, half), row_idx),
            pl.BlockSpec((1, R, half), row_idx),
            pl.BlockSpec((1, 4, D), lambda b, i: (b, 0, 0)),
            pl.BlockSpec((half, D), const2, pipeline_mode=pl.Buffered(1)),
            pl.BlockSpec((half, D), const2, pipeline_mode=pl.Buffered(1)),
            pl.BlockSpec((1, D), const2),
            pl.BlockSpec((D, dff), const2, pipeline_mode=pl.Buffered(1)),
            pl.BlockSpec((dff, D), const2, pipeline_mode=pl.Buffered(1)),
            pl.BlockSpec((1, D), const2),
        ],
        out_specs=pl.BlockSpec((1, R, D), row_idx),
        out_shape=jax.ShapeDtypeStruct((B, T, D), F32),
        compiler_params=pltpu.CompilerParams(dimension_semantics=("arbitrary", "arbitrary"),
                                             vmem_limit_bytes=VMEM_LIMIT),
        name="out_mlp",
    )(x, o_mla, o_hg, mod4, woa, wob, nmlp, w1, w2, fnorm)


def _rope_slot(w):
    z = jnp.zeros(w.shape[:-1] + (MLA_NOPE,), w.dtype)
    z2 = jnp.zeros(w.shape[:-1] + (HEAD_SLOT - MLA_NOPE - MLA_ROPE,), w.dtype)
    return jnp.concatenate([z, w, z2], axis=-1)


def _pack_w_in(w_in):
    w = w_in.astype(BF16)
    n_mla = MLA_Q_RANK + MLA_KV_RANK
    return jnp.concatenate([w[:, :n_mla], _rope_slot(w[:, n_mla:n_mla + MLA_ROPE]), w[:, n_mla + MLA_ROPE:]], axis=1)


def _pack_w_uq(w_uq):
    k = w_uq.shape[0]
    w3 = w_uq.reshape(k, MLA_HEADS, MLA_NOPE + MLA_ROPE)
    nope = w3[..., :MLA_NOPE].reshape(k, MLA_HEADS * MLA_NOPE)
    rope = w3[..., MLA_NOPE:].reshape(k, MLA_HEADS // 2, 2 * MLA_ROPE)
    rope = jnp.concatenate([jnp.zeros((k, MLA_HEADS // 2, LANES - 2 * MLA_ROPE), w_uq.dtype), rope], axis=-1)
    return jnp.concatenate([nope, rope.reshape(k, (MLA_HEADS // 2) * LANES)], axis=1).astype(BF16)


def _pack_w_ukv(w_ukv):
    k = w_ukv.shape[0]
    w3 = w_ukv.reshape(k, MLA_HEADS, MLA_NOPE + MLA_V)
    kk = w3[..., :MLA_NOPE].reshape(k, MLA_HEADS * MLA_NOPE)
    vv = w3[..., MLA_NOPE:].reshape(k, MLA_HEADS * MLA_V)
    return jnp.concatenate([kk, vv], axis=1).astype(BF16)


def _rope_tables(T):
    rows = T // GRID_W
    row = np.repeat(np.arange(rows), GRID_W).astype(np.float64)
    col = np.tile(np.arange(GRID_W), rows).astype(np.float64)
    n = MLA_ROPE // 4
    inv = ROPE_BASE ** (-np.arange(n, dtype=np.float64) / n)
    ar, ac = row[:, None] * inv, col[:, None] * inv
    cos32 = np.concatenate([np.cos(ar), np.cos(ar), np.cos(ac), np.cos(ac)], axis=-1)
    sin32 = np.concatenate([np.sin(ar), np.sin(ar), np.sin(ac), np.sin(ac)], axis=-1)
    cos_t = np.concatenate([np.ones((T, MLA_NOPE)), cos32, cos32], axis=-1)
    sin_t = np.concatenate([np.zeros((T, MLA_NOPE)), sin32, sin32], axis=-1)
    return jnp.asarray(cos_t, F32), jnp.asarray(sin_t, F32)


def _scan_constants(reverse):
    R = ROW_TILE
    t = np.arange(R, dtype=np.int32)[:, None]
    s = np.arange(R, dtype=np.int32)[None, :]
    if reverse:
        t, s = s, t
    tri = (t >= s)
    H = R // 2
    t, s = np.broadcast_to(t, (R, R))[:H, :H], np.broadcast_to(s, (R, R))[:H, :H]
    x = t ^ s
    lev = np.zeros((H, H), np.int32)
    for bit in range(1, N_LEVELS - 1):
        lev = np.where((x >> bit) > 0, bit, lev)
    lv = np.where(t > s, lev, np.where(t == s, N_LEVELS - 1, -1))
    return jnp.asarray(tri, BF16), jnp.asarray(lv, jnp.int32)


def kernel(x, c, ctx, c_ctx, w_ada, b_ada, norm_mix, w_in, q_norm, w_uq, kv_norm, w_ukv, hgrn_lb, hgrn_norm,
           w_out, norm_mlp, w_mlp_in, w_mlp_out, final_norm):
    B, T, D = x.shape
    L = ctx.shape[1]
    assert w_ada.shape[0] == 1, "single-layer block"
    assert T % ROW_TILE == 0 and L == ROW_TILE and T % GRID_W == 0
    assert T % MLP_ROW_TILE == 0 and T % ATTN_ROW_TILE == 0 and T % PROJ_ROW_TILE == 0
    assert PROJ_ROW_TILE % L == 0 and B % (PROJ_ROW_TILE // L) == 0 and T % L == 0 and B % HGRN_SAMPLES == 0

    pad_rows = (-(B + 1)) % 8
    cc = jnp.concatenate([c, c_ctx[None, :], jnp.zeros((pad_rows, D), F32)], axis=0)
    mod = _ada_call(cc, w_ada[0], b_ada[0][None, :])
    mod_lat = mod[:B].reshape(B, 6, D)
    mod_ctx = mod[B].reshape(1, 6, D)
    mod4 = mod_lat[:, 2:6]

    cos_t, sin_t = _rope_tables(T)
    win_p, wkv_p = _pack_w_in(w_in[0]), _pack_w_ukv(w_ukv[0])
    q, k, v, qh, lff, lfb, vi, g = _inproj_lat_call(
        x, mod_lat[:, 0:2], norm_mix, win_p, q_norm, _pack_w_uq(w_uq[0]), kv_norm, wkv_p, hgrn_lb, cos_t, sin_t)
    tri_f, lv_f = _scan_constants(False)
    tri_b, lv_b = _scan_constants(True)
    kc, vc, st_f, st_b = _inproj_ctx_call(
        ctx, mod_ctx[:, 0:2], norm_mix, win_p, kv_norm, wkv_p, hgrn_lb, tri_f, tri_b)

    o_f = _hgrn_call(False, (qh, lff, vi), st_f, tri_f, lv_f)
    o_hg = _hgrn_call(True, (qh, lfb, vi), st_b, tri_b, lv_b, extra=(o_f, g, hgrn_norm))

    o_mla = _attn_call(q, k, kc, v, vc)

    half = MLA_HEADS * MLA_V
    wo = w_out[0].astype(BF16)
    return _mlp_call(x, o_mla, o_hg, mod4, wo[:half], wo[half:], norm_mlp, w_mlp_in[0].astype(BF16),
                     w_mlp_out[0].astype(BF16), final_norm[None, :])
```
